```python
import math
import jax, jax.numpy as jnp
from jax import lax
import numpy as np

D_MODEL = 1024
BATCH = 4
SEQ = 8192
DEPTH = 1
DEC_BATCH = 16
DEC_SEQ = 64
PAST_LEN = 1024

CHUNK = 64
RET_HEADS = 4
RET_DK = 128
RET_DV = 256
SB_HEADS = 4
SB_D = 128
Q_BLOCK = 128
N_KEYS = 128
N_EXPERTS = N_KEYS * N_KEYS
PEER_HEADS = 8
PEER_DKEY = 256
PEER_HALF = PEER_DKEY // 2
PEER_TOPK = 16
PEER_BLOCK = 128
ROPE_BASE = 10000.0
EPS = 1e-6
N_BRANCH = 2
RET_QK_W = RET_HEADS * RET_DK
RET_V_W = RET_HEADS * RET_DV
SB_W = SB_HEADS * SB_D
SPLITS = [RET_QK_W, RET_QK_W, RET_V_W, RET_V_W, SB_W, SB_W, SB_W, N_BRANCH * D_MODEL]
IN_W = sum(SPLITS)

kernel_name = "retention_stickbreak_peer_stream_step"

F32 = jnp.float32


def rms_norm(x, g):
    xf = x.astype(F32)
    y = xf * lax.rsqrt(jnp.mean(xf * xf, axis=-1, keepdims=True) + EPS)
    return (y * g.astype(F32)).astype(x.dtype)


def rotary(x, pos):
    d = x.shape[-1]
    inv = ROPE_BASE ** (-jnp.arange(0, d, 2, dtype=F32) / d)
    ang = pos.astype(F32)[:, None] * inv[None, :]
    cos = jnp.cos(ang)[None, :, None, :]
    sin = jnp.sin(ang)[None, :, None, :]
    xf = x.astype(F32)
    x1, x2 = xf[..., : d // 2], xf[..., d // 2:]
    return jnp.concatenate([x1 * cos - x2 * sin, x1 * sin + x2 * cos], axis=-1).astype(x.dtype)


def retention_log_decay():
    return jnp.log(1.0 - 2.0 ** (-5.0 - jnp.arange(RET_HEADS, dtype=F32)))


def retention_chunk(S, q, k, v):
    q = q.astype(F32)
    k = k.astype(F32)
    v = v.astype(F32)
    S = S.astype(F32)
    T = q.shape[1]
    log_g = retention_log_decay()
    i = jnp.arange(T, dtype=F32)
    dist = jnp.abs(i[:, None] - i[None, :])
    dmat = jnp.exp(dist[None] * log_g[:, None, None])
    scores = jnp.einsum('bqhd,bkhd->bhqk', q, k) * dmat[None]
    o = jnp.einsum('bhqk,bkhe->bqhe', scores, v)
    inter = jnp.exp((i[:, None] + 1.0) * log_g[None, :])
    o = o + jnp.einsum('bqhd,bhde->bqhe', q, S) * inter[None, :, :, None]
    kdec = jnp.exp((T - 1.0 - i)[:, None] * log_g[None, :])
    S_new = jnp.exp(T * log_g)[None, :, None, None] * S + jnp.einsum('bkhd,kh,bkhe->bhde', k, kdec, v)
    return o, S_new


def sb_attend(q, k, v, q_pos, k_pos):
    z = jnp.einsum('bqhd,bkhd->bhqk', q, k).astype(F32) * (SB_D ** -0.5)
    mask = (k_pos[None, :] < q_pos[:, None])[None, None]
    log_fail = jnp.where(mask, jax.nn.log_sigmoid(-z), 0.0)
    after = lax.cumsum(log_fail, axis=3, reverse=True) - log_fail
    w = jnp.where(mask, jnp.exp(jax.nn.log_sigmoid(z) + after), 0.0)
    return jnp.einsum('bhqk,bkhd->bqhd', w.astype(v.dtype), v)


def project_inputs(h, w_in, pos):
    B, T, _ = h.shape
    p = h @ w_in
    offs = np.cumsum(SPLITS)[:-1].tolist()
    rq, rk, rv, rg, sq, sk, sv, gt = jnp.split(p, offs, axis=-1)
    rq = rotary(rq.reshape(B, T, RET_HEADS, RET_DK), pos)
    rk = rotary(rk.reshape(B, T, RET_HEADS, RET_DK), pos) * (RET_DK ** -0.5)
    rv = rv.reshape(B, T, RET_HEADS, RET_DV)
    sq = sq.reshape(B, T, SB_HEADS, SB_D)
    sk = sk.reshape(B, T, SB_HEADS, SB_D)
    sv = sv.reshape(B, T, SB_HEADS, SB_D)
    gates = jax.nn.sigmoid(gt.reshape(B, T, N_BRANCH, D_MODEL).astype(F32)).astype(h.dtype)
    return rq, rk, rv, rg, sq, sk, sv, gates


def peer_ffn(h, w_pq, pk_keys, peer_u, peer_v):
    B, T, D = h.shape
    x = h.reshape(-1, D)
    n = x.shape[0]
    n_pad = (-n) % PEER_BLOCK
    xb_all = jnp.pad(x, ((0, n_pad), (0, 0))).reshape(-1, PEER_BLOCK, D)

    def block(xb):
        q = (xb @ w_pq).reshape(PEER_BLOCK, PEER_HEADS, 2, PEER_HALF)
        s = jnp.einsum('nhpc,phkc->nhpk', q, pk_keys).astype(F32)
        v1, i1 = lax.top_k(s[:, :, 0], PEER_TOPK)
        v2, i2 = lax.top_k(s[:, :, 1], PEER_TOPK)
        cand = (v1[..., :, None] + v2[..., None, :]).reshape(PEER_BLOCK, PEER_HEADS, PEER_TOPK * PEER_TOPK)
        cidx = (i1[..., :, None] * N_KEYS + i2[..., None, :]).reshape(PEER_BLOCK, PEER_HEADS, PEER_TOPK * PEER_TOPK)
        sc, sel = lax.top_k(cand, PEER_TOPK)
        eidx = jnp.take_along_axis(cidx, sel, axis=-1)
        g = jax.nn.softmax(sc, axis=-1)
        u = peer_u[eidx]
        a = jax.nn.gelu(jnp.einsum('nhkd,nd->nhk', u, xb).astype(F32))
        coef = (g * a).astype(peer_v.dtype)
        return jnp.einsum('nhk,nhkd->nd', coef, peer_v[eidx])

    out = lax.map(block, xb_all).reshape(-1, D)[:n]
    return out.reshape(B, T, D).astype(h.dtype)


def merge_and_channel_mix(x, ret_o, rg, sb_o, gates, ret_gn_w, w_ret_o, w_sb_o, w_out,
                          norm2_w, w_pq, pk_keys, peer_u, peer_v):
    B, T, _ = x.shape
    mu = jnp.mean(ret_o, axis=-1, keepdims=True)
    var = jnp.mean((ret_o - mu) ** 2, axis=-1, keepdims=True)
    r = ((ret_o - mu) * lax.rsqrt(var + EPS)).reshape(B, T, RET_V_W) * ret_gn_w.astype(F32)
    r = (jax.nn.silu(rg.astype(F32)) * r).astype(x.dtype)
    ret_branch = r @ w_ret_o
    sb_branch = sb_o.reshape(B, T, SB_W).astype(x.dtype) @ w_sb_o
    mixed = gates[:, :, 0] * ret_branch + gates[:, :, 1] * sb_branch
    x = x + mixed @ w_out
    x = x + peer_ffn(rms_norm(x, norm2_w), w_pq, pk_keys, peer_u, peer_v)
    return x


def prompt_layer(x, norm1_w, w_in, ret_gn_w, w_ret_o, w_sb_o, w_out, norm2_w, w_pq, pk_keys, peer_u, peer_v):
    B, T, _ = x.shape
    pos = jnp.arange(T, dtype=jnp.int32)
    h = rms_norm(x, norm1_w)
    rq, rk, rv, rg, sq, sk, sv, gates = project_inputs(h, w_in, pos)
    nc = T // CHUNK

    def to_chunks(a):
        return a.reshape(B, nc, CHUNK, *a.shape[2:]).swapaxes(0, 1)

    def step(S, qkv):
        o, S_new = retention_chunk(S, *qkv)
        return S_new, o

    S0 = jnp.zeros((B, RET_HEADS, RET_DK, RET_DV), F32)
    S_fin, ret_o = lax.scan(step, S0, (to_chunks(rq), to_chunks(rk), to_chunks(rv)))
    ret_o = ret_o.swapaxes(0, 1).reshape(B, T, RET_HEADS, RET_DV)
    nb = T // Q_BLOCK
    qb = sq.reshape(B, nb, Q_BLOCK, SB_HEADS, SB_D).swapaxes(0, 1)
    pb = pos.reshape(nb, Q_BLOCK)
    sb_o = lax.map(lambda a: sb_attend(a[0], sk, sv, a[1], pos), (qb, pb))
    sb_o = sb_o.swapaxes(0, 1).reshape(B, T, SB_HEADS, SB_D)
    y = merge_and_channel_mix(x, ret_o, rg, sb_o, gates, ret_gn_w, w_ret_o, w_sb_o, w_out,
                              norm2_w, w_pq, pk_keys, peer_u, peer_v)
    return y, sk, sv, S_fin.astype(x.dtype)


def sample_layer(x, past_k, past_v, S_past, norm1_w, w_in, ret_gn_w, w_ret_o, w_sb_o, w_out,
                 norm2_w, w_pq, pk_keys, peer_u, peer_v):
    B, T, _ = x.shape
    P = past_k.shape[1]
    pos = P + jnp.arange(T, dtype=jnp.int32)
    h = rms_norm(x, norm1_w)
    rq, rk, rv, rg, sq, sk, sv, gates = project_inputs(h, w_in, pos)
    ret_o, S_new = retention_chunk(S_past, rq, rk, rv)
    k_all = jnp.concatenate([past_k.astype(sk.dtype), sk], axis=1)
    v_all = jnp.concatenate([past_v.astype(sv.dtype), sv], axis=1)
    k_pos = jnp.arange(P + T, dtype=jnp.int32)
    sb_o = sb_attend(sq, k_all, v_all, pos, k_pos)
    y = merge_and_channel_mix(x, ret_o, rg, sb_o, gates, ret_gn_w, w_ret_o, w_sb_o, w_out,
                              norm2_w, w_pq, pk_keys, peer_u, peer_v)
    return y, sk, sv, S_new.astype(S_past.dtype)


def setup_inputs(seed: int = 0) -> dict:
    key = jax.random.key(seed)
    ks = jax.random.split(key, 20)
    nrm = jax.random.normal
    D = D_MODEL
    return {
        "x_prompt": nrm(ks[0], (BATCH, SEQ, D), F32),
        "x_sample": nrm(ks[1], (DEC_BATCH, DEC_SEQ, D), F32),
        "cache_sb_k": nrm(ks[2], (DEPTH, DEC_BATCH, PAST_LEN, SB_HEADS, SB_D), F32),
        "cache_sb_v": nrm(ks[3], (DEPTH, DEC_BATCH, PAST_LEN, SB_HEADS, SB_D), F32),
        "state_ret": 0.1 * nrm(ks[4], (DEPTH, DEC_BATCH, RET_HEADS, RET_DK, RET_DV), F32),
        "norm1_w": 1.0 + 0.01 * nrm(ks[5], (DEPTH, D), F32),
        "w_in": nrm(ks[6], (DEPTH, D, IN_W), F32) * D ** -0.5,
        "ret_gn_w": 1.0 + 0.01 * nrm(ks[7], (DEPTH, RET_V_W), F32),
        "w_ret_o": nrm(ks[8], (DEPTH, RET_V_W, D), F32) * RET_V_W ** -0.5,
        "w_sb_o": nrm(ks[9], (DEPTH, SB_W, D), F32) * SB_W ** -0.5,
        "w_out": nrm(ks[10], (DEPTH, D, D), F32) * D ** -0.5,
        "norm2_w": 1.0 + 0.01 * nrm(ks[11], (DEPTH, D), F32),
        "w_pq": nrm(ks[12], (DEPTH, D, PEER_HEADS * PEER_DKEY), F32) * D ** -0.5,
        "pk_keys": nrm(ks[13], (DEPTH, 2, PEER_HEADS, N_KEYS, PEER_HALF), F32) * PEER_HALF ** -0.5,
        "peer_u": nrm(ks[14], (DEPTH, N_EXPERTS, D), F32) * D ** -0.5,
        "peer_v": nrm(ks[15], (DEPTH, N_EXPERTS, D), F32) * PEER_HEADS ** -0.5,
        "norm_f_w": 1.0 + 0.01 * nrm(ks[16], (D,), F32),
    }


def reference(x_prompt, x_sample, cache_sb_k, cache_sb_v, state_ret, norm1_w, w_in, ret_gn_w,
              w_ret_o, w_sb_o, w_out, norm2_w, w_pq, pk_keys, peer_u, peer_v, norm_f_w):
    xp = x_prompt
    xs = x_sample
    pk_list, pv_list, ps_list = [], [], []
    sk_list, sv_list, ss_list = [], [], []
    for l in range(DEPTH):
        xp, kp, vp, Sp = prompt_layer(xp, norm1_w[l], w_in[l], ret_gn_w[l], w_ret_o[l], w_sb_o[l],
                                      w_out[l], norm2_w[l], w_pq[l], pk_keys[l], peer_u[l], peer_v[l])
        xs, ksn, vsn, Ss = sample_layer(xs, cache_sb_k[l], cache_sb_v[l], state_ret[l], norm1_w[l], w_in[l],
                                        ret_gn_w[l], w_ret_o[l], w_sb_o[l], w_out[l], norm2_w[l],
                                        w_pq[l], pk_keys[l], peer_u[l], peer_v[l])
        pk_list.append(kp)
        pv_list.append(vp)
        ps_list.append(Sp)
        sk_list.append(ksn)
        sv_list.append(vsn)
        ss_list.append(Ss)
    y_prompt = rms_norm(xp, norm_f_w)
    y_sample = rms_norm(xs, norm_f_w)
    prompt_sb_k = jnp.stack(pk_list)
    prompt_sb_v = jnp.stack(pv_list)
    prompt_ret_state = jnp.stack(ps_list)
    sample_sb_k = jnp.stack(sk_list)
    sample_sb_v = jnp.stack(sv_list)
    sample_ret_state = jnp.stack(ss_list)
    return (y_prompt, y_sample, prompt_sb_k, prompt_sb_v, prompt_ret_state, sample_sb_k, sample_sb_v, sample_ret_state)
```

```python
import functools
import math

import numpy as np
import jax
import jax.numpy as jnp
from jax import lax
from jax.experimental import pallas as pl
from jax.experimental.pallas import tpu as pltpu

F32 = jnp.float32
BF16 = jnp.bfloat16

D_MODEL = 1024
CHUNK = 64
RET_HEADS = 4
RET_DK = 128
RET_DV = 256
SB_HEADS = 4
SB_D = 128
N_KEYS = 128
N_EXPERTS = N_KEYS * N_KEYS
PEER_HEADS = 8
PEER_HALF = 128
PEER_TOPK = 16
ROPE_BASE = 10000.0
EPS = 1e-6
RET_QK_W = RET_HEADS * RET_DK
RET_V_W = RET_HEADS * RET_DV
SB_W = SB_HEADS * SB_D
IN_W = 2 * RET_QK_W + 2 * RET_V_W + 3 * SB_W + 2 * D_MODEL

VMEM_LIMIT_BYTES = 56 * 1024 * 1024
NEG_BIG = -1e30


def _pick_tile(n, candidates):
    for c in candidates:
        if n % c == 0:
            return c
    raise ValueError(f"no tile in {candidates} divides {n}")


def _params(*sem):
    return pltpu.CompilerParams(dimension_semantics=sem, vmem_limit_bytes=VMEM_LIMIT_BYTES)


def _in_proj_kernel(x_ref, g_ref, w_ref, cos_ref, sin_ref,
                    rq_ref, rk_ref, rv_ref, rg_ref, sq_ref, sk_ref, sv_ref, skb_ref, svb_ref, gt_ref):
    x = x_ref[...]
    h = (x * lax.rsqrt(jnp.mean(x * x, axis=-1, keepdims=True) + EPS) * g_ref[...]).astype(BF16)

    def proj(lo, width):
        return jnp.dot(h, w_ref[:, lo:lo + width], preferred_element_type=F32)

    c = cos_ref[...]
    s = sin_ref[...]

    def rotary_store(p, out_ref, scale):
        for hh in range(RET_HEADS):
            ph = p[:, hh * RET_DK:(hh + 1) * RET_DK]
            r = ph * c + pltpu.roll(ph, RET_DK // 2, 1) * s
            if scale != 1.0:
                r = r * scale
            out_ref[:, hh * RET_DK:(hh + 1) * RET_DK] = r.astype(BF16)

    rotary_store(proj(0, RET_QK_W), rq_ref, 1.0)
    rotary_store(proj(RET_QK_W, RET_QK_W), rk_ref, RET_DK ** -0.5)
    off = 2 * RET_QK_W
    for j in range(RET_V_W // 512):
        rv_ref[:, j * 512:(j + 1) * 512] = proj(off + j * 512, 512).astype(BF16)
    off += RET_V_W
    for j in range(RET_V_W // 512):
        g = proj(off + j * 512, 512)
        rg_ref[:, j * 512:(j + 1) * 512] = (g / (1.0 + jnp.exp(-g))).astype(BF16)
    off += RET_V_W
    sq_ref[...] = (proj(off, SB_W) * SB_D ** -0.5).astype(BF16)
    off += SB_W
    k = proj(off, SB_W)
    sk_ref[...] = k
    skb_ref[...] = k.astype(BF16)
    off += SB_W
    v = proj(off, SB_W)
    sv_ref[...] = v
    svb_ref[...] = v.astype(BF16)
    off += SB_W
    for j in range(2 * D_MODEL // 512):
        g = proj(off + j * 512, 512)
        gt_ref[:, j * 512:(j + 1) * 512] = (1.0 / (1.0 + jnp.exp(-g))).astype(BF16)


def _in_proj(x2d, norm_w, w_in_bf, cos_tab, sin_tab):
    n = x2d.shape[0]
    tm = _pick_tile(n, (512, 256, 128, 64))
    assert cos_tab.shape[0] % tm == 0
    npos = cos_tab.shape[0] // tm
    row = lambda i: (i, 0)
    const = lambda i: (0, 0)
    pos = lambda i: (i % npos, 0)
    widths = [(RET_QK_W, BF16), (RET_QK_W, BF16), (RET_V_W, BF16), (RET_V_W, BF16), (SB_W, BF16),
              (SB_W, F32), (SB_W, F32), (SB_W, BF16), (SB_W, BF16), (2 * D_MODEL, BF16)]
    return pl.pallas_call(
        _in_proj_kernel,
        grid=(n // tm,),
        in_specs=[pl.BlockSpec((tm, D_MODEL), row),
                  pl.BlockSpec((1, D_MODEL), const),
                  pl.BlockSpec((D_MODEL, IN_W), const),
                  pl.BlockSpec((tm, RET_DK), pos),
                  pl.BlockSpec((tm, RET_DK), pos)],
        out_specs=[pl.BlockSpec((tm, w), row) for w, _ in widths],
        out_shape=[jax.ShapeDtypeStruct((n, w), dt) for w, dt in widths],
        compiler_params=_params("parallel"),
        name="in_proj",
    )(x2d, norm_w.reshape(1, D_MODEL), w_in_bf, cos_tab, sin_tab)


def _rotary_tables(pos):
    inv = ROPE_BASE ** (-jnp.arange(0, RET_DK, 2, dtype=F32) / RET_DK)
    ang = pos.astype(F32)[:, None] * inv[None, :]
    cos = jnp.cos(ang)
    sin = jnp.sin(ang)
    return jnp.concatenate([cos, cos], axis=1), jnp.concatenate([-sin, sin], axis=1)


def _retention_tables(blk):
    log_g = np.log(1.0 - 2.0 ** (-5.0 - np.arange(RET_HEADS, dtype=np.float64)))
    i = np.arange(blk, dtype=np.float64)
    dist = np.abs(i[:, None] - i[None, :])
    visible = (np.arange(blk)[None, :] // CHUNK) <= (np.arange(blk)[:, None] // CHUNK)
    dmat = np.where(visible[None], np.exp(dist[None] * log_g[:, None, None]), 0.0)
    inter = np.exp((i[None, :] + 1.0) * log_g[:, None])
    kdec = np.exp((blk - 1.0 - i)[None, :] * log_g[:, None])
    inter = np.broadcast_to(inter[:, :, None], (RET_HEADS, blk, RET_DV))
    kdec = np.broadcast_to(kdec[:, :, None], (RET_HEADS, blk, RET_DK))
    sdec = tuple(float(np.exp(blk * g)) for g in log_g)
    return (jnp.asarray(dmat, F32), jnp.asarray(inter, F32), jnp.asarray(kdec, F32)), sdec


def _retention_kernel(sdec, q_ref, k_ref, v_ref, gate_ref, s0_ref, dmat_ref, inter_ref, kdec_ref, gn_ref,
                      r_ref, sfin_ref, state_ref):
    t = pl.program_id(1)

    @pl.when(t == 0)
    def _():
        state_ref[...] = s0_ref[0]

    for hh in range(RET_HEADS):
        q = q_ref[:, hh * RET_DK:(hh + 1) * RET_DK]
        k = k_ref[:, hh * RET_DK:(hh + 1) * RET_DK]
        v = v_ref[:, hh * RET_DV:(hh + 1) * RET_DV]
        state = state_ref[hh]
        sc = lax.dot_general(q, k, (((1,), (1,)), ((), ())), preferred_element_type=F32)
        p = (sc * dmat_ref[hh]).astype(BF16)
        o = jnp.dot(p, v, preferred_element_type=F32)
        o = o + jnp.dot(q, state.astype(BF16), preferred_element_type=F32) * inter_ref[hh]
        kd = (k.astype(F32) * kdec_ref[hh]).astype(BF16)
        upd = lax.dot_general(kd, v, (((0,), (0,)), ((), ())), preferred_element_type=F32)
        state_ref[hh] = sdec[hh] * state + upd
        mu = jnp.mean(o, axis=-1, keepdims=True)
        cen = o - mu
        var = jnp.mean(cen * cen, axis=-1, keepdims=True)
        r = cen * lax.rsqrt(var + EPS) * gn_ref[:, hh * RET_DV:(hh + 1) * RET_DV]
        r = r * gate_ref[:, hh * RET_DV:(hh + 1) * RET_DV].astype(F32)
        r_ref[:, hh * RET_DV:(hh + 1) * RET_DV] = r.astype(BF16)

    @pl.when(t == pl.num_programs(1) - 1)
    def _():
        sfin_ref[0] = state_ref[...]


def _retention(rq, rk, rv, gate, s0, gn_w, batch, seq):
    blk = _pick_tile(seq, (256, 128, 64))
    nt = seq // blk
    tabs, sdec = _retention_tables(blk)
    row = lambda b, t: (b * nt + t, 0)
    per_b = lambda b, t: (b, 0, 0, 0)
    const3 = lambda b, t: (0, 0, 0)
    return pl.pallas_call(
        functools.partial(_retention_kernel, sdec),
        grid=(batch, nt),
        in_specs=[pl.BlockSpec((blk, RET_QK_W), row),
                  pl.BlockSpec((blk, RET_QK_W), row),
                  pl.BlockSpec((blk, RET_V_W), row),
                  pl.BlockSpec((blk, RET_V_W), row),
                  pl.BlockSpec((1, RET_HEADS, RET_DK, RET_DV), per_b),
                  pl.BlockSpec((RET_HEADS, blk, blk), const3),
                  pl.BlockSpec((RET_HEADS, blk, RET_DV), const3),
                  pl.BlockSpec((RET_HEADS, blk, RET_DK), const3),
                  pl.BlockSpec((1, RET_V_W), lambda b, t: (0, 0))],
        out_specs=[pl.BlockSpec((blk, RET_V_W), row),
                   pl.BlockSpec((1, RET_HEADS, RET_DK, RET_DV), per_b)],
        out_shape=[jax.ShapeDtypeStruct((batch * seq, RET_V_W), BF16),
                   jax.ShapeDtypeStruct((batch, RET_HEADS, RET_DK, RET_DV), F32)],
        scratch_shapes=[pltpu.VMEM((RET_HEADS, RET_DK, RET_DV), F32)],
        compiler_params=_params("parallel", "arbitrary"),
        name="retention",
    )(rq, rk, rv, gate, s0, *tabs, gn_w.reshape(1, RET_V_W))


SB_BLOCK = 256


def _suffix_matrix(n):
    return jnp.asarray(np.tril(np.ones((n, n), np.float32), -1), BF16)


def _sb_block(q, kblk, vblk, umat, carry, mask):
    z = lax.dot_general(q, kblk, (((1,), (1,)), ((), ())), preferred_element_type=F32)
    sp = jnp.maximum(z, 0.0) + jnp.log(1.0 + jnp.exp(-jnp.abs(z)))
    spm = sp if mask is None else jnp.where(mask, sp, 0.0)
    after = jnp.dot(spm.astype(BF16), umat, preferred_element_type=F32)
    w = jnp.exp(z - sp - after - carry)
    if mask is not None:
        w = jnp.where(mask, w, 0.0)
    out = jnp.dot(w.astype(BF16), vblk, preferred_element_type=F32)
    return out, carry + after[:, 0:1] + spm[:, 0:1]


def _causal_mask(n):
    r = lax.broadcasted_iota(jnp.int32, (n, n), 0)
    c = lax.broadcasted_iota(jnp.int32, (n, n), 1)
    return c < r


def _sb_prompt_kernel(q_ref, k_ref, v_ref, u_ref, o_ref, acc_ref, carry_ref):
    qi = pl.program_id(2)
    blk = q_ref.shape[0]
    q = q_ref[...]
    umat = u_ref[...]
    start = pl.multiple_of(qi * blk, blk)
    out, carry = _sb_block(q, k_ref[pl.ds(start, blk), :], v_ref[pl.ds(start, blk), :], umat,
                           jnp.zeros((blk, 1), F32), _causal_mask(blk))
    acc_ref[...] = out
    carry_ref[...] = carry

    def body(i, _):
        kj = qi - 1 - i
        s = pl.multiple_of(kj * blk, blk)
        out, carry = _sb_block(q, k_ref[pl.ds(s, blk), :], v_ref[pl.ds(s, blk), :], umat,
                               carry_ref[...], None)
        acc_ref[...] += out
        carry_ref[...] = carry
        return 0

    lax.fori_loop(0, qi, body, 0)
    o_ref[...] = acc_ref[...].astype(BF16)


def _sb_prompt(sq, skb, svb, batch, seq):
    blk = _pick_tile(seq, (SB_BLOCK, 128, 64))
    nq = seq // blk
    qmap = lambda b, h, i: (b * nq + i, h)
    kvmap = lambda b, h, i: (b, h)
    return pl.pallas_call(
        _sb_prompt_kernel,
        grid=(batch, SB_HEADS, nq),
        in_specs=[pl.BlockSpec((blk, SB_D), qmap),
                  pl.BlockSpec((seq, SB_D), kvmap),
                  pl.BlockSpec((seq, SB_D), kvmap),
                  pl.BlockSpec((blk, blk), lambda b, h, i: (0, 0))],
        out_specs=pl.BlockSpec((blk, SB_D), qmap),
        out_shape=jax.ShapeDtypeStruct((batch * seq, SB_W), BF16),
        scratch_shapes=[pltpu.VMEM((blk, SB_D), F32), pltpu.VMEM((blk, 1), F32)],
        compiler_params=_params("parallel", "parallel", "arbitrary"),
        name="sb_prompt",
    )(sq, skb, svb, _suffix_matrix(blk))


def _sb_sample_kernel(pblk, q_ref, k_ref, v_ref, pk_ref, pv_ref, u_ref, o_ref):
    t = q_ref.shape[0]
    past = pk_ref.shape[0]
    q = q_ref[...]
    acc, carry = _sb_block(q, k_ref[...], v_ref[...], u_ref[:t, :t], jnp.zeros((t, 1), F32), _causal_mask(t))
    for j in reversed(range(past // pblk)):
        kb = pk_ref[j * pblk:(j + 1) * pblk, :].astype(BF16)
        vb = pv_ref[j * pblk:(j + 1) * pblk, :].astype(BF16)
        out, carry = _sb_block(q, kb, vb, u_ref[:pblk, :pblk], carry, None)
        acc = acc + out
    o_ref[...] = acc.astype(BF16)


def _sb_sample(sq, skb, svb, past_k, past_v, batch, seq, past):
    pblk = _pick_tile(past, (SB_BLOCK, 128, 64))
    ublk = max(pblk, seq)
    cur = lambda b, h: (b, h)
    return pl.pallas_call(
        functools.partial(_sb_sample_kernel, pblk),
        grid=(batch, SB_HEADS),
        in_specs=[pl.BlockSpec((seq, SB_D), cur),
                  pl.BlockSpec((seq, SB_D), cur),
                  pl.BlockSpec((seq, SB_D), cur),
                  pl.BlockSpec((past, SB_D), cur),
                  pl.BlockSpec((past, SB_D), cur),
                  pl.BlockSpec((ublk, ublk), lambda b, h: (0, 0))],
        out_specs=pl.BlockSpec((seq, SB_D), cur),
        out_shape=jax.ShapeDtypeStruct((batch * seq, SB_W), BF16),
        compiler_params=_params("parallel", "parallel"),
        name="sb_sample",
    )(sq, skb, svb, past_k, past_v, _suffix_matrix(ublk))


def _merge_kernel(x_ref, r_ref, sb_ref, gt_ref, wr_ref, ws_ref, wo_ref, g2_ref, x1_ref, h2t_ref):
    rb = jnp.dot(r_ref[...], wr_ref[...], preferred_element_type=F32)
    sb = jnp.dot(sb_ref[...], ws_ref[...], preferred_element_type=F32)
    mixed = gt_ref[:, :D_MODEL].astype(F32) * rb + gt_ref[:, D_MODEL:].astype(F32) * sb
    x1 = x_ref[...] + jnp.dot(mixed.astype(BF16), wo_ref[...], preferred_element_type=F32)
    x1_ref[...] = x1
    h2 = x1 * lax.rsqrt(jnp.mean(x1 * x1, axis=-1, keepdims=True) + EPS) * g2_ref[...]
    h2t_ref[...] = h2.T.astype(BF16)


def _merge(x2d, r, sbo, gates, w_ret_o, w_sb_o, w_out, norm2_w):
    n = x2d.shape[0]
    tm = _pick_tile(n, (512, 256, 128))
    row = lambda i: (i, 0)
    const = lambda i: (0, 0)
    return pl.pallas_call(
        _merge_kernel,
        grid=(n // tm,),
        in_specs=[pl.BlockSpec((tm, D_MODEL), row),
                  pl.BlockSpec((tm, RET_V_W), row),
                  pl.BlockSpec((tm, SB_W), row),
                  pl.BlockSpec((tm, 2 * D_MODEL), row),
                  pl.BlockSpec((RET_V_W, D_MODEL), const),
                  pl.BlockSpec((SB_W, D_MODEL), const),
                  pl.BlockSpec((D_MODEL, D_MODEL), const),
                  pl.BlockSpec((1, D_MODEL), const)],
        out_specs=[pl.BlockSpec((tm, D_MODEL), row),
                   pl.BlockSpec((D_MODEL, tm), lambda i: (0, i))],
        out_shape=[jax.ShapeDtypeStruct((n, D_MODEL), F32),
                   jax.ShapeDtypeStruct((D_MODEL, n), BF16)],
        compiler_params=_params("parallel"),
        name="merge",
    )(x2d, r, sbo, gates, w_ret_o, w_sb_o, w_out, norm2_w.reshape(1, D_MODEL))


PEER_SLAB = 1024


def _sort_desc(a):
    a = list(a)
    n = len(a)
    k = 2
    while k <= n:
        j = k // 2
        while j >= 1:
            for i in range(n):
                l = i ^ j
                if l > i:
                    hi = jnp.maximum(a[i], a[l])
                    lo = jnp.minimum(a[i], a[l])
                    a[i], a[l] = (hi, lo) if (i & k) == 0 else (lo, hi)
            j //= 2
        k *= 2
    return a


def _merge_bitonic_desc(a):
    a = list(a)
    n = len(a)
    j = n // 2
    while j >= 1:
        for i in range(n):
            l = i ^ j
            if l > i:
                a[i], a[l] = jnp.maximum(a[i], a[l]), jnp.minimum(a[i], a[l])
        j //= 2
    return a


def _top_merge(a, b):
    n = len(a)
    return _merge_bitonic_desc([jnp.maximum(a[i], b[n - 1 - i]) for i in range(n)])


def _top16_rows(s):
    groups = s.shape[0] // 8
    a = _sort_desc([s[8 * j:8 * j + 8, :] for j in range(groups)])
    for shift in (4, 2, 1):
        a = _top_merge(a, [pltpu.roll(x, shift, 0) for x in a])
    return a


def _peer_kernel(h2t_ref, x1_ref, wq_ref, keys_ref, u_ref, vt_ref, gf_ref, y_ref,
                 s1_ref, s2_ref, c1_ref, tau_ref, acc_ref):
    j = pl.program_id(1)
    tn = h2t_ref.shape[1]
    ht = h2t_ref[...]

    @pl.when(j == 0)
    def _():
        sub = lax.broadcasted_iota(jnp.int32, (8, tn), 0)
        tops = []
        for p, dst in ((0, s1_ref), (1, s2_ref)):
            packed = [jnp.zeros((8, tn), F32) for _ in range(PEER_TOPK)]
            for hh in range(PEER_HEADS):
                r0 = (hh * 2 + p) * PEER_HALF
                qt = jnp.dot(wq_ref[r0:r0 + PEER_HALF, :], ht, preferred_element_type=F32)
                s = jnp.dot(keys_ref[p, hh], qt.astype(BF16), preferred_element_type=F32)
                dst[hh] = s
                top = _top16_rows(s)
                packed = [jnp.where(sub == hh, top[a], packed[a]) for a in range(PEER_TOPK)]
            tops.append(packed)
        v1, v2 = tops
        cands = [v1[a] + v2[b] for a in range(PEER_TOPK) for b in range(PEER_TOPK // (a + 1))]
        pad = [jnp.full((8, tn), NEG_BIG, F32)] * (-len(cands) % PEER_TOPK)
        padded = cands + pad
        lists = [_sort_desc(padded[i:i + PEER_TOPK]) for i in range(0, len(padded), PEER_TOPK)]
        while len(lists) > 1:
            lists = [_top_merge(lists[i], lists[i + 1]) for i in range(0, len(lists), 2)]
        tau = lists[0][PEER_TOPK - 1]
        m = cands[0]
        z = jnp.zeros((8, tn), F32)
        for cnd in cands:
            z = z + jnp.where(cnd >= tau, jnp.exp(cnd - m), 0.0)
        mz = m + jnp.log(z)
        tau_ref[...] = tau
        for hh in range(PEER_HEADS):
            c1_ref[hh] = s1_ref[hh] - mz[hh:hh + 1, :]
        acc_ref[...] = jnp.zeros_like(acc_ref)

    a_t = jnp.dot(u_ref[...], ht, preferred_element_type=F32)
    n_i1 = u_ref.shape[0] // N_KEYS
    coefs = []
    for ii in range(n_i1):
        i1 = j * n_i1 + ii
        g = jnp.zeros((N_KEYS, tn), F32)
        for hh in range(PEER_HEADS):
            s1row = s1_ref[hh, pl.ds(i1, 1), :]
            c1row = c1_ref[hh, pl.ds(i1, 1), :]
            s2 = s2_ref[hh]
            sel = (s1row + s2) >= tau_ref[hh:hh + 1, :]
            g = g + jnp.where(sel, jnp.exp(c1row + s2), 0.0)
        act = jax.nn.gelu(a_t[ii * N_KEYS:(ii + 1) * N_KEYS, :])
        coefs.append((g * act).astype(BF16))
    coef = jnp.concatenate(coefs, axis=0)
    acc_ref[...] += jnp.dot(vt_ref[...], coef, preferred_element_type=F32)

    @pl.when(j == pl.num_programs(1) - 1)
    def _():
        x2 = x1_ref[...] + acc_ref[...].T
        y = x2 * lax.rsqrt(jnp.mean(x2 * x2, axis=-1, keepdims=True) + EPS) * gf_ref[...]
        y_ref[...] = y


def _peer(h2t, x1, wq_t, keys_bf, u_bf, vt_bf, norm_f_w):
    n = x1.shape[0]
    tn = _pick_tile(n, (512, 256, 128))
    ns = N_EXPERTS // PEER_SLAB
    return pl.pallas_call(
        _peer_kernel,
        grid=(n // tn, ns),
        in_specs=[pl.BlockSpec((D_MODEL, tn), lambda i, j: (0, i)),
                  pl.BlockSpec((tn, D_MODEL), lambda i, j: (i, 0)),
                  pl.BlockSpec((2 * PEER_HEADS * PEER_HALF, D_MODEL), lambda i, j: (0, 0)),
                  pl.BlockSpec((2, PEER_HEADS, N_KEYS, PEER_HALF), lambda i, j: (0, 0, 0, 0)),
                  pl.BlockSpec((PEER_SLAB, D_MODEL), lambda i, j: (j, 0)),
                  pl.BlockSpec((D_MODEL, PEER_SLAB), lambda i, j: (0, j)),
                  pl.BlockSpec((1, D_MODEL), lambda i, j: (0, 0))],
        out_specs=pl.BlockSpec((tn, D_MODEL), lambda i, j: (i, 0)),
        out_shape=jax.ShapeDtypeStruct((n, D_MODEL), F32),
        scratch_shapes=[pltpu.VMEM((PEER_HEADS, N_KEYS, tn), F32),
                        pltpu.VMEM((PEER_HEADS, N_KEYS, tn), F32),
                        pltpu.VMEM((PEER_HEADS, N_KEYS, tn), F32),
                        pltpu.VMEM((PEER_HEADS, tn), F32),
                        pltpu.VMEM((D_MODEL, tn), F32)],
        compiler_params=_params("parallel", "arbitrary"),
        name="peer",
    )(h2t, x1, wq_t, keys_bf, u_bf, vt_bf, norm_f_w.reshape(1, D_MODEL))


def _layer(x, pos, past_k, past_v, s0, wts):
    batch, seq, _ = x.shape
    n = batch * seq
    x2d = x.reshape(n, D_MODEL)
    cos_tab, sin_tab = _rotary_tables(pos)
    tm = _pick_tile(n, (512, 256, 128, 64))
    if seq < tm:
        cos_tab = jnp.tile(cos_tab, (tm // seq, 1))
        sin_tab = jnp.tile(sin_tab, (tm // seq, 1))
    rq, rk, rv, gate, sq, sk, sv, skb, svb, gates = _in_proj(x2d, wts["norm1_w"], wts["w_in"], cos_tab, sin_tab)
    r, s_fin = _retention(rq, rk, rv, gate, s0, wts["ret_gn_w"], batch, seq)
    if past_k is None:
        sbo = _sb_prompt(sq, skb, svb, batch, seq)
    else:
        sbo = _sb_sample(sq, skb, svb, past_k, past_v, batch, seq, past_k.shape[0] // batch)
    x1, h2t = _merge(x2d, r, sbo, gates, wts["w_ret_o"], wts["w_sb_o"], wts["w_out"], wts["norm2_w"])
    y = _peer(h2t, x1, wts["w_pq_t"], wts["pk_keys"], wts["peer_u"], wts["peer_v_t"], wts["norm_f_w"])
    return (y.reshape(batch, seq, D_MODEL), sk.reshape(batch, seq, SB_HEADS, SB_D),
            sv.reshape(batch, seq, SB_HEADS, SB_D), s_fin)


def kernel(x_prompt, x_sample, cache_sb_k, cache_sb_v, state_ret, norm1_w, w_in, ret_gn_w, w_ret_o, w_sb_o,
           w_out, norm2_w, w_pq, pk_keys, peer_u, peer_v, norm_f_w):
    assert norm1_w.shape[0] == 1, "single-layer model"
    wts = dict(
        norm1_w=norm1_w[0], w_in=w_in[0].astype(BF16), ret_gn_w=ret_gn_w[0],
        w_ret_o=w_ret_o[0].astype(BF16), w_sb_o=w_sb_o[0].astype(BF16), w_out=w_out[0].astype(BF16),
        norm2_w=norm2_w[0], w_pq_t=w_pq[0].T.astype(BF16), pk_keys=pk_keys[0].astype(BF16),
        peer_u=peer_u[0].astype(BF16), peer_v_t=peer_v[0].T.astype(BF16), norm_f_w=norm_f_w)
    bp, tp, _ = x_prompt.shape
    bs, ts, _ = x_sample.shape
    past = cache_sb_k.shape[2]
    yp, kp, vp, sp = _layer(x_prompt, jnp.arange(tp, dtype=jnp.int32), None, None,
                            jnp.zeros((bp, RET_HEADS, RET_DK, RET_DV), F32), wts)
    ys, ks, vs, ss = _layer(x_sample, past + jnp.arange(ts, dtype=jnp.int32),
                            cache_sb_k[0].reshape(bs * past, SB_W), cache_sb_v[0].reshape(bs * past, SB_W),
                            state_ret[0], wts)
    return (yp, ys, kp[None], vp[None], sp[None], ks[None], vs[None], ss[None])
```

```python
import functools
import math

import numpy as np
import jax
import jax.numpy as jnp
from jax import lax
from jax.experimental import pallas as pl
from jax.experimental.pallas import tpu as pltpu

F32 = jnp.float32
BF16 = jnp.bfloat16

D_MODEL = 1024
CHUNK = 64
RET_HEADS = 4
RET_DK = 128
RET_DV = 256
SB_HEADS = 4
SB_D = 128
N_KEYS = 128
N_EXPERTS = N_KEYS * N_KEYS
PEER_HEADS = 8
PEER_HALF = 128
PEER_TOPK = 16
ROPE_BASE = 10000.0
EPS = 1e-6
RET_QK_W = RET_HEADS * RET_DK
RET_V_W = RET_HEADS * RET_DV
SB_W = SB_HEADS * SB_D
IN_W = 2 * RET_QK_W + 2 * RET_V_W + 3 * SB_W + 2 * D_MODEL

VMEM_LIMIT_BYTES = 56 * 1024 * 1024
NEG_BIG = -1e30


def _pick_tile(n, candidates):
    for c in candidates:
        if n % c == 0:
            return c
    raise ValueError(f"no tile in {candidates} divides {n}")


def _params(*sem):
    return pltpu.CompilerParams(dimension_semantics=sem, vmem_limit_bytes=VMEM_LIMIT_BYTES)


def _in_proj_kernel(x_ref, g_ref, w_ref, cos_ref, sin_ref,
                    rq_ref, rk_ref, rv_ref, rg_ref, sq_ref, sk_ref, sv_ref, skb_ref, svb_ref, gt_ref):
    x = x_ref[...]
    h = (x * lax.rsqrt(jnp.mean(x * x, axis=-1, keepdims=True) + EPS) * g_ref[...]).astype(BF16)

    def proj(lo, width):
        return jnp.dot(h, w_ref[:, lo:lo + width], preferred_element_type=F32)

    c = cos_ref[...]
    s = sin_ref[...]

    def rotary_store(p, out_ref, scale):
        for hh in range(RET_HEADS):
            ph = p[:, hh * RET_DK:(hh + 1) * RET_DK]
            r = ph * c + pltpu.roll(ph, RET_DK // 2, 1) * s
            if scale != 1.0:
                r = r * scale
            out_ref[:, hh * RET_DK:(hh + 1) * RET_DK] = r.astype(BF16)

    rotary_store(proj(0, RET_QK_W), rq_ref, 1.0)
    rotary_store(proj(RET_QK_W, RET_QK_W), rk_ref, RET_DK ** -0.5)
    off = 2 * RET_QK_W
    for j in range(RET_V_W // 512):
        rv_ref[:, j * 512:(j + 1) * 512] = proj(off + j * 512, 512).astype(BF16)
    off += RET_V_W
    for j in range(RET_V_W // 512):
        g = proj(off + j * 512, 512)
        rg_ref[:, j * 512:(j + 1) * 512] = (g / (1.0 + jnp.exp(-g))).astype(BF16)
    off += RET_V_W
    sq_ref[...] = (proj(off, SB_W) * SB_D ** -0.5).astype(BF16)
    off += SB_W
    k = proj(off, SB_W)
    sk_ref[...] = k
    skb_ref[...] = k.astype(BF16)
    off += SB_W
    v = proj(off, SB_W)
    sv_ref[...] = v
    svb_ref[...] = v.astype(BF16)
    off += SB_W
    for j in range(2 * D_MODEL // 512):
        g = proj(off + j * 512, 512)
        gt_ref[:, j * 512:(j + 1) * 512] = (1.0 / (1.0 + jnp.exp(-g))).astype(BF16)


def _in_proj(x2d, norm_w, w_in_bf, cos_tab, sin_tab):
    n = x2d.shape[0]
    tm = _pick_tile(n, (512, 256, 128, 64))
    assert cos_tab.shape[0] % tm == 0
    npos = cos_tab.shape[0] // tm
    row = lambda i: (i, 0)
    const = lambda i: (0, 0)
    pos = lambda i: (i % npos, 0)
    widths = [(RET_QK_W, BF16), (RET_QK_W, BF16), (RET_V_W, BF16), (RET_V_W, BF16), (SB_W, BF16),
              (SB_W, F32), (SB_W, F32), (SB_W, BF16), (SB_W, BF16), (2 * D_MODEL, BF16)]
    return pl.pallas_call(
        _in_proj_kernel,
        grid=(n // tm,),
        in_specs=[pl.BlockSpec((tm, D_MODEL), row),
                  pl.BlockSpec((1, D_MODEL), const),
                  pl.BlockSpec((D_MODEL, IN_W), const),
                  pl.BlockSpec((tm, RET_DK), pos),
                  pl.BlockSpec((tm, RET_DK), pos)],
        out_specs=[pl.BlockSpec((tm, w), row) for w, _ in widths],
        out_shape=[jax.ShapeDtypeStruct((n, w), dt) for w, dt in widths],
        compiler_params=_params("parallel"),
        name="in_proj",
    )(x2d, norm_w.reshape(1, D_MODEL), w_in_bf, cos_tab, sin_tab)


def _rotary_tables(pos):
    inv = ROPE_BASE ** (-jnp.arange(0, RET_DK, 2, dtype=F32) / RET_DK)
    ang = pos.astype(F32)[:, None] * inv[None, :]
    cos = jnp.cos(ang)
    sin = jnp.sin(ang)
    return jnp.concatenate([cos, cos], axis=1), jnp.concatenate([-sin, sin], axis=1)


def _retention_tables(blk):
    log_g = np.log(1.0 - 2.0 ** (-5.0 - np.arange(RET_HEADS, dtype=np.float64)))
    i = np.arange(blk, dtype=np.float64)
    dist = np.abs(i[:, None] - i[None, :])
    visible = (np.arange(blk)[None, :] // CHUNK) <= (np.arange(blk)[:, None] // CHUNK)
    dmat = np.where(visible[None], np.exp(dist[None] * log_g[:, None, None]), 0.0)
    inter = np.exp((i[None, :] + 1.0) * log_g[:, None])
    kdec = np.exp((blk - 1.0 - i)[None, :] * log_g[:, None])
    inter = np.broadcast_to(inter[:, :, None], (RET_HEADS, blk, RET_DV))
    kdec = np.broadcast_to(kdec[:, :, None], (RET_HEADS, blk, RET_DK))
    sdec = tuple(float(np.exp(blk * g)) for g in log_g)
    return (jnp.asarray(dmat, F32), jnp.asarray(inter, F32), jnp.asarray(kdec, F32)), sdec


def _retention_kernel(sdec, q_ref, k_ref, v_ref, gate_ref, s0_ref, dmat_ref, inter_ref, kdec_ref, gn_ref,
                      r_ref, sfin_ref, state_ref):
    t = pl.program_id(1)

    @pl.when(t == 0)
    def _():
        state_ref[...] = s0_ref[0]

    for hh in range(RET_HEADS):
        q = q_ref[:, hh * RET_DK:(hh + 1) * RET_DK]
        k = k_ref[:, hh * RET_DK:(hh + 1) * RET_DK]
        v = v_ref[:, hh * RET_DV:(hh + 1) * RET_DV]
        state = state_ref[hh]
        sc = lax.dot_general(q, k, (((1,), (1,)), ((), ())), preferred_element_type=F32)
        p = (sc * dmat_ref[hh]).astype(BF16)
        o = jnp.dot(p, v, preferred_element_type=F32)
        o = o + jnp.dot(q, state.astype(BF16), preferred_element_type=F32) * inter_ref[hh]
        kd = (k.astype(F32) * kdec_ref[hh]).astype(BF16)
        upd = lax.dot_general(kd, v, (((0,), (0,)), ((), ())), preferred_element_type=F32)
        state_ref[hh] = sdec[hh] * state + upd
        mu = jnp.mean(o, axis=-1, keepdims=True)
        cen = o - mu
        var = jnp.mean(cen * cen, axis=-1, keepdims=True)
        r = cen * lax.rsqrt(var + EPS) * gn_ref[:, hh * RET_DV:(hh + 1) * RET_DV]
        r = r * gate_ref[:, hh * RET_DV:(hh + 1) * RET_DV].astype(F32)
        r_ref[:, hh * RET_DV:(hh + 1) * RET_DV] = r.astype(BF16)

    @pl.when(t == pl.num_programs(1) - 1)
    def _():
        sfin_ref[0] = state_ref[...]


def _retention(rq, rk, rv, gate, s0, gn_w, batch, seq):
    blk = _pick_tile(seq, (256, 128, 64))
    nt = seq // blk
    tabs, sdec = _retention_tables(blk)
    row = lambda b, t: (b * nt + t, 0)
    per_b = lambda b, t: (b, 0, 0, 0)
    const3 = lambda b, t: (0, 0, 0)
    return pl.pallas_call(
        functools.partial(_retention_kernel, sdec),
        grid=(batch, nt),
        in_specs=[pl.BlockSpec((blk, RET_QK_W), row),
                  pl.BlockSpec((blk, RET_QK_W), row),
                  pl.BlockSpec((blk, RET_V_W), row),
                  pl.BlockSpec((blk, RET_V_W), row),
                  pl.BlockSpec((1, RET_HEADS, RET_DK, RET_DV), per_b),
                  pl.BlockSpec((RET_HEADS, blk, blk), const3),
                  pl.BlockSpec((RET_HEADS, blk, RET_DV), const3),
                  pl.BlockSpec((RET_HEADS, blk, RET_DK), const3),
                  pl.BlockSpec((1, RET_V_W), lambda b, t: (0, 0))],
        out_specs=[pl.BlockSpec((blk, RET_V_W), row),
                   pl.BlockSpec((1, RET_HEADS, RET_DK, RET_DV), per_b)],
        out_shape=[jax.ShapeDtypeStruct((batch * seq, RET_V_W), BF16),
                   jax.ShapeDtypeStruct((batch, RET_HEADS, RET_DK, RET_DV), F32)],
        scratch_shapes=[pltpu.VMEM((RET_HEADS, RET_DK, RET_DV), F32)],
        compiler_params=_params("parallel", "arbitrary"),
        name="retention",
    )(rq, rk, rv, gate, s0, *tabs, gn_w.reshape(1, RET_V_W))


SB_BLOCK = 256
SB_UNROLL = 4
SOFTPLUS_CLAMP = 30.0


def _suffix_matrix(n):
    return jnp.asarray(np.tril(np.ones((n, n), np.float32), -1), BF16)


def _sb_block(q, kblk, vblk, umat, carry, mask):
    z = lax.dot_general(q, kblk, (((1,), (1,)), ((), ())), preferred_element_type=F32)
    sp = jnp.maximum(jnp.log(1.0 + jnp.exp(jnp.minimum(z, SOFTPLUS_CLAMP))), z)
    spm = sp if mask is None else jnp.where(mask, sp, 0.0)
    after = jnp.dot(spm.astype(BF16), umat, preferred_element_type=F32)
    w = jnp.exp(z - sp - after - carry)
    if mask is not None:
        w = jnp.where(mask, w, 0.0)
    out = jnp.dot(w.astype(BF16), vblk, preferred_element_type=F32)
    return out, carry + after[:, 0:1] + spm[:, 0:1]


def _causal_mask(n):
    r = lax.broadcasted_iota(jnp.int32, (n, n), 0)
    c = lax.broadcasted_iota(jnp.int32, (n, n), 1)
    return c < r


def _sb_prompt_kernel(q_ref, k_ref, v_ref, u_ref, o_ref, acc_ref, carry_ref):
    qi = pl.program_id(2)
    blk = q_ref.shape[0]
    q = q_ref[...]
    umat = u_ref[...]
    start = pl.multiple_of(qi * blk, blk)
    out, carry = _sb_block(q, k_ref[pl.ds(start, blk), :], v_ref[pl.ds(start, blk), :], umat,
                           jnp.zeros((blk, 1), F32), _causal_mask(blk))
    acc_ref[...] = out
    carry_ref[...] = carry

    def sweep(first, count):
        carry = carry_ref[...]
        total = None
        for u in range(count):
            s = pl.multiple_of((first - u) * blk, blk)
            out, carry = _sb_block(q, k_ref[pl.ds(s, blk), :], v_ref[pl.ds(s, blk), :], umat, carry, None)
            total = out if total is None else total + out
        acc_ref[...] += total
        carry_ref[...] = carry

    def main_body(i, _):
        sweep(qi - 1 - i * SB_UNROLL, SB_UNROLL)
        return 0

    def tail_body(i, _):
        sweep(qi % SB_UNROLL - 1 - i, 1)
        return 0

    lax.fori_loop(0, qi // SB_UNROLL, main_body, 0)
    lax.fori_loop(0, qi % SB_UNROLL, tail_body, 0)
    o_ref[...] = acc_ref[...].astype(BF16)


def _sb_prompt(sq, skb, svb, batch, seq):
    blk = _pick_tile(seq, (SB_BLOCK, 128, 64))
    nq = seq // blk
    qmap = lambda b, h, i: (b * nq + i, h)
    kvmap = lambda b, h, i: (b, h)
    return pl.pallas_call(
        _sb_prompt_kernel,
        grid=(batch, SB_HEADS, nq),
        in_specs=[pl.BlockSpec((blk, SB_D), qmap),
                  pl.BlockSpec((seq, SB_D), kvmap),
                  pl.BlockSpec((seq, SB_D), kvmap),
                  pl.BlockSpec((blk, blk), lambda b, h, i: (0, 0))],
        out_specs=pl.BlockSpec((blk, SB_D), qmap),
        out_shape=jax.ShapeDtypeStruct((batch * seq, SB_W), BF16),
        scratch_shapes=[pltpu.VMEM((blk, SB_D), F32), pltpu.VMEM((blk, 1), F32)],
        compiler_params=_params("parallel", "parallel", "arbitrary"),
        name="sb_prompt",
    )(sq, skb, svb, _suffix_matrix(blk))


def _sb_sample_kernel(pblk, q_ref, k_ref, v_ref, pk_ref, pv_ref, u_ref, o_ref):
    t = q_ref.shape[0]
    past = pk_ref.shape[0]
    q = q_ref[...]
    acc, carry = _sb_block(q, k_ref[...], v_ref[...], u_ref[:t, :t], jnp.zeros((t, 1), F32), _causal_mask(t))
    for j in reversed(range(past // pblk)):
        kb = pk_ref[j * pblk:(j + 1) * pblk, :].astype(BF16)
        vb = pv_ref[j * pblk:(j + 1) * pblk, :].astype(BF16)
        out, carry = _sb_block(q, kb, vb, u_ref[:pblk, :pblk], carry, None)
        acc = acc + out
    o_ref[...] = acc.astype(BF16)


def _sb_sample(sq, skb, svb, past_k, past_v, batch, seq, past):
    pblk = _pick_tile(past, (SB_BLOCK, 128, 64))
    ublk = max(pblk, seq)
    cur = lambda b, h: (b, h)
    return pl.pallas_call(
        functools.partial(_sb_sample_kernel, pblk),
        grid=(batch, SB_HEADS),
        in_specs=[pl.BlockSpec((seq, SB_D), cur),
                  pl.BlockSpec((seq, SB_D), cur),
                  pl.BlockSpec((seq, SB_D), cur),
                  pl.BlockSpec((past, SB_D), cur),
                  pl.BlockSpec((past, SB_D), cur),
                  pl.BlockSpec((ublk, ublk), lambda b, h: (0, 0))],
        out_specs=pl.BlockSpec((seq, SB_D), cur),
        out_shape=jax.ShapeDtypeStruct((batch * seq, SB_W), BF16),
        compiler_params=_params("parallel", "parallel"),
        name="sb_sample",
    )(sq, skb, svb, past_k, past_v, _suffix_matrix(ublk))


def _merge_kernel(x_ref, r_ref, sb_ref, gt_ref, wr_ref, ws_ref, wo_ref, g2_ref, x1_ref, h2t_ref):
    rb = jnp.dot(r_ref[...], wr_ref[...], preferred_element_type=F32)
    sb = jnp.dot(sb_ref[...], ws_ref[...], preferred_element_type=F32)
    mixed = gt_ref[:, :D_MODEL].astype(F32) * rb + gt_ref[:, D_MODEL:].astype(F32) * sb
    x1 = x_ref[...] + jnp.dot(mixed.astype(BF16), wo_ref[...], preferred_element_type=F32)
    x1_ref[...] = x1
    h2 = x1 * lax.rsqrt(jnp.mean(x1 * x1, axis=-1, keepdims=True) + EPS) * g2_ref[...]
    h2t_ref[...] = h2.T.astype(BF16)


def _merge(x2d, r, sbo, gates, w_ret_o, w_sb_o, w_out, norm2_w):
    n = x2d.shape[0]
    tm = _pick_tile(n, (512, 256, 128))
    row = lambda i: (i, 0)
    const = lambda i: (0, 0)
    return pl.pallas_call(
        _merge_kernel,
        grid=(n // tm,),
        in_specs=[pl.BlockSpec((tm, D_MODEL), row),
                  pl.BlockSpec((tm, RET_V_W), row),
                  pl.BlockSpec((tm, SB_W), row),
                  pl.BlockSpec((tm, 2 * D_MODEL), row),
                  pl.BlockSpec((RET_V_W, D_MODEL), const),
                  pl.BlockSpec((SB_W, D_MODEL), const),
                  pl.BlockSpec((D_MODEL, D_MODEL), const),
                  pl.BlockSpec((1, D_MODEL), const)],
        out_specs=[pl.BlockSpec((tm, D_MODEL), row),
                   pl.BlockSpec((D_MODEL, tm), lambda i: (0, i))],
        out_shape=[jax.ShapeDtypeStruct((n, D_MODEL), F32),
                   jax.ShapeDtypeStruct((D_MODEL, n), BF16)],
        compiler_params=_params("parallel"),
        name="merge",
    )(x2d, r, sbo, gates, w_ret_o, w_sb_o, w_out, norm2_w.reshape(1, D_MODEL))


PEER_SLAB = 1024


def _sort_desc(a):
    a = list(a)
    n = len(a)
    k = 2
    while k <= n:
        j = k // 2
        while j >= 1:
            for i in range(n):
                l = i ^ j
                if l > i:
                    hi = jnp.maximum(a[i], a[l])
                    lo = jnp.minimum(a[i], a[l])
                    a[i], a[l] = (hi, lo) if (i & k) == 0 else (lo, hi)
            j //= 2
        k *= 2
    return a


def _merge_bitonic_desc(a):
    a = list(a)
    n = len(a)
    j = n // 2
    while j >= 1:
        for i in range(n):
            l = i ^ j
            if l > i:
                a[i], a[l] = jnp.maximum(a[i], a[l]), jnp.minimum(a[i], a[l])
        j //= 2
    return a


def _top_merge(a, b):
    n = len(a)
    return _merge_bitonic_desc([jnp.maximum(a[i], b[n - 1 - i]) for i in range(n)])


def _top16_rows(s):
    groups = s.shape[0] // 8
    a = _sort_desc([s[8 * j:8 * j + 8, :] for j in range(groups)])
    for shift in (4, 2, 1):
        a = _top_merge(a, [pltpu.roll(x, shift, 0) for x in a])
    return a


def _lookup_desc(keys, vals, default, s):
    rows, tn = s.shape
    s3 = s.reshape(rows // 8, 8, tn)
    out = jnp.full_like(s3, default)
    for b in reversed(range(len(keys))):
        v = vals[b] if isinstance(vals[b], float) else vals[b][None]
        out = jnp.where(s3 == keys[b][None], v, out)
    return out.reshape(rows, tn)


def _peer_kernel(n_slabs, h2t_ref, x1_ref, wq_ref, keys_ref, u_ref, vt_ref, gf_ref, y_ref,
                 s1_ref, cnt_ref, e1_ref, rank2_ref, e2_ref, coef0_ref, coef1_ref, gate0_ref, gate1_ref,
                 acc_ref):
    j = pl.program_id(1)
    ns = n_slabs
    coef_refs = (coef0_ref, coef1_ref)
    gate_refs = (gate0_ref, gate1_ref)
    tn = h2t_ref.shape[1]
    ht = h2t_ref[...]

    @pl.when(j == 0)
    def _():
        sub = lax.broadcasted_iota(jnp.int32, (8, tn), 0)

        def scores(p, hh):
            r0 = (hh * 2 + p) * PEER_HALF
            qt = jnp.dot(wq_ref[r0:r0 + PEER_HALF, :], ht, preferred_element_type=F32)
            return jnp.dot(keys_ref[p, hh], qt.astype(BF16), preferred_element_type=F32)

        v1 = [jnp.zeros((8, tn), F32) for _ in range(PEER_TOPK)]
        v2 = [jnp.zeros((8, tn), F32) for _ in range(PEER_TOPK)]
        for hh in range(PEER_HEADS):
            s1 = scores(0, hh)
            s1_ref[hh] = s1
            top = _top16_rows(s1)
            v1 = [jnp.where(sub == hh, top[a], v1[a]) for a in range(PEER_TOPK)]
            s2 = scores(1, hh)
            top = _top16_rows(s2)
            v2 = [jnp.where(sub == hh, top[a], v2[a]) for a in range(PEER_TOPK)]
            ranks = [float(b) for b in range(PEER_TOPK)]
            rank2_ref[hh] = _lookup_desc(top, ranks, float(PEER_TOPK), s2).astype(BF16)
            e2_ref[hh] = jnp.exp(s2.reshape(N_KEYS // 8, 8, tn) - top[0][None]).reshape(N_KEYS, tn).astype(BF16)
        cand_rows = [[v1[a] + v2[b] for b in range(PEER_TOPK // (a + 1))] for a in range(PEER_TOPK)]
        cands = [c for row in cand_rows for c in row]
        pad = [jnp.full((8, tn), NEG_BIG, F32)] * (-len(cands) % PEER_TOPK)
        padded = cands + pad
        lists = [_sort_desc(padded[i:i + PEER_TOPK]) for i in range(0, len(padded), PEER_TOPK)]
        while len(lists) > 1:
            lists = [_top_merge(lists[i], lists[i + 1]) for i in range(0, len(lists), 2)]
        tau = lists[0][PEER_TOPK - 1]
        m = cands[0]
        z = jnp.zeros((8, tn), F32)
        for cnd in cands:
            z = z + jnp.where(cnd >= tau, jnp.exp(cnd - m), 0.0)
        shift = v1[0] + jnp.log(z)
        cnt_rows = []
        for row in cand_rows:
            n_sel = jnp.zeros((8, tn), F32)
            for cnd in row:
                n_sel = n_sel + jnp.where(cnd >= tau, 1.0, 0.0)
            cnt_rows.append(n_sel)
        for hh in range(PEER_HEADS):
            s1 = s1_ref[hh]
            head_row = lambda x: jnp.broadcast_to(x[hh:hh + 1, :], (8, tn))
            cnt_ref[hh] = _lookup_desc([head_row(v) for v in v1], [head_row(c) for c in cnt_rows], 0.0, s1)
            e1_ref[hh] = jnp.exp(s1 - shift[hh:hh + 1, :])
        acc_ref[...] = jnp.zeros_like(acc_ref)
        coef_refs[1][...] = jnp.zeros_like(coef_refs[1])

    n_i1 = u_ref.shape[0] // N_KEYS
    pack = 16

    def gate_rows(slab):
        out = []
        for ii in range(n_i1):
            i1 = jnp.minimum(slab * n_i1 + ii, N_KEYS - 1)
            out.append([(cnt_ref[hh, pl.ds(i1, 1), :], e1_ref[hh, pl.ds(i1, 1), :])
                        for hh in range(PEER_HEADS)])
        return out

    def gate_group(dst_ref, table_rows, ii):
        cnt_b = [jnp.broadcast_to(c, (pack, tn)).astype(BF16) for c, _ in table_rows[ii]]
        e1_b = [jnp.broadcast_to(e, (pack, tn)).astype(BF16) for _, e in table_rows[ii]]
        for c in range(N_KEYS // pack):
            rows = slice(c * pack, (c + 1) * pack)
            g = jnp.zeros((pack, tn), BF16)
            for hh in range(PEER_HEADS):
                w = e2_ref[hh, rows, :] * e1_b[hh]
                g = g + jnp.where(rank2_ref[hh, rows, :] < cnt_b[hh], w, jnp.zeros_like(w))
            r0 = ii * N_KEYS + c * pack
            dst_ref[r0:r0 + pack, :] = g

    @pl.when(j == 0)
    def _():
        rows0 = gate_rows(0)
        for ii in range(n_i1):
            gate_group(gate_refs[0], rows0, ii)

    def add_previous_slab(prev_ref):
        acc_ref[...] += jnp.dot(vt_ref[...], prev_ref[...], preferred_element_type=F32)

    def build_slab(parity):
        cur_ref, prev_ref = coef_refs[parity], coef_refs[1 - parity]
        gate_ref, next_gate_ref = gate_refs[parity], gate_refs[1 - parity]
        next_rows = gate_rows(j + 1)
        n_split = 2 if tn % 512 == 0 else 1
        wn = tn // n_split
        m_split = 2
        em = u_ref.shape[0] // m_split
        dm = D_MODEL // m_split
        gm = em // N_KEYS
        assert n_i1 == m_split * gm

        dots = {}

        def expert_dot_quadrant(mi, ni):
            dots[mi, ni] = jnp.dot(u_ref[mi * em:(mi + 1) * em, :], h2t_ref[:, ni * wn:(ni + 1) * wn],
                                   preferred_element_type=F32)

        def previous_slab_quadrant(mi, ni):
            blk = (slice(mi * dm, (mi + 1) * dm), slice(ni * wn, (ni + 1) * wn))
            acc_ref[blk] += jnp.dot(vt_ref[mi * dm:(mi + 1) * dm, :], prev_ref[:, ni * wn:(ni + 1) * wn],
                                    preferred_element_type=F32)

        def coef_group(ii):
            mi, local = divmod(ii, gm)
            for c in range(N_KEYS // pack):
                for ni in range(n_split):
                    lr = local * N_KEYS + c * pack
                    act = jax.nn.gelu(dots[mi, ni][lr:lr + pack, :]).astype(BF16)
                    r0 = ii * N_KEYS + c * pack
                    cols = slice(ni * wn, (ni + 1) * wn)
                    cur_ref[r0:r0 + pack, cols] = gate_ref[r0:r0 + pack, cols] * act

        quads = [(mi, ni) for mi in range(m_split) for ni in range(n_split)]
        matmuls = ([functools.partial(expert_dot_quadrant, *q) for q in quads]
                   + [functools.partial(previous_slab_quadrant, *q) for q in quads])
        tasks = [[] for _ in matmuls]
        first_coef_slot = n_split
        for ii in range(n_i1):
            slot = min(first_coef_slot + ii // 2, len(tasks) - 1)
            slot = max(slot, (ii // gm + 1) * n_split)
            tasks[min(slot, len(tasks) - 1)].append(functools.partial(coef_group, ii))
        for ii in range(n_i1):
            tasks[ii * len(tasks) // n_i1].append(functools.partial(gate_group, next_gate_ref, next_rows, ii))
        for mm, work in zip(matmuls, tasks):
            mm()
            for w in work:
                w()

    for parity in range(2):
        @pl.when(jnp.logical_and(j < ns, j % 2 == parity))
        def _():
            build_slab(parity)

    @pl.when(j == ns)
    def _():
        add_previous_slab(coef_refs[(n_slabs - 1) % 2])
        x2 = x1_ref[...] + acc_ref[...].T
        y = x2 * lax.rsqrt(jnp.mean(x2 * x2, axis=-1, keepdims=True) + EPS) * gf_ref[...]
        y_ref[...] = y


def _peer(h2t, x1, wq_t, keys_bf, u_bf, vt_bf, norm_f_w):
    n = x1.shape[0]
    tn = _pick_tile(n, (512, 256, 128))
    ns = N_EXPERTS // PEER_SLAB
    table = (PEER_HEADS, N_KEYS, tn)
    return pl.pallas_call(
        functools.partial(_peer_kernel, ns),
        grid=(n // tn, ns + 1),
        in_specs=[pl.BlockSpec((D_MODEL, tn), lambda i, j: (0, i)),
                  pl.BlockSpec((tn, D_MODEL), lambda i, j: (i, 0)),
                  pl.BlockSpec((2 * PEER_HEADS * PEER_HALF, D_MODEL), lambda i, j: (0, 0)),
                  pl.BlockSpec((2, PEER_HEADS, N_KEYS, PEER_HALF), lambda i, j: (0, 0, 0, 0)),
                  pl.BlockSpec((PEER_SLAB, D_MODEL), lambda i, j: (jnp.minimum(j, ns - 1), 0)),
                  pl.BlockSpec((D_MODEL, PEER_SLAB), lambda i, j: (0, jnp.maximum(j - 1, 0))),
                  pl.BlockSpec((1, D_MODEL), lambda i, j: (0, 0))],
        out_specs=pl.BlockSpec((tn, D_MODEL), lambda i, j: (i, 0)),
        out_shape=jax.ShapeDtypeStruct((n, D_MODEL), F32),
        scratch_shapes=[pltpu.VMEM(table, F32),
                        pltpu.VMEM(table, F32),
                        pltpu.VMEM(table, F32),
                        pltpu.VMEM(table, BF16),
                        pltpu.VMEM(table, BF16),
                        pltpu.VMEM((PEER_SLAB, tn), BF16),
                        pltpu.VMEM((PEER_SLAB, tn), BF16),
                        pltpu.VMEM((PEER_SLAB, tn), BF16),
                        pltpu.VMEM((PEER_SLAB, tn), BF16),
                        pltpu.VMEM((D_MODEL, tn), F32)],
        compiler_params=_params("parallel", "arbitrary"),
        name="peer",
    )(h2t, x1, wq_t, keys_bf, u_bf, vt_bf, norm_f_w.reshape(1, D_MODEL))


def _layer(x, pos, past_k, past_v, s0, wts):
    batch, seq, _ = x.shape
    n = batch * seq
    x2d = x.reshape(n, D_MODEL)
    cos_tab, sin_tab = _rotary_tables(pos)
    tm = _pick_tile(n, (512, 256, 128, 64))
    if seq < tm:
        cos_tab = jnp.tile(cos_tab, (tm // seq, 1))
        sin_tab = jnp.tile(sin_tab, (tm // seq, 1))
    rq, rk, rv, gate, sq, sk, sv, skb, svb, gates = _in_proj(x2d, wts["norm1_w"], wts["w_in"], cos_tab, sin_tab)
    r, s_fin = _retention(rq, rk, rv, gate, s0, wts["ret_gn_w"], batch, seq)
    if past_k is None:
        sbo = _sb_prompt(sq, skb, svb, batch, seq)
    else:
        sbo = _sb_sample(sq, skb, svb, past_k, past_v, batch, seq, past_k.shape[0] // batch)
    x1, h2t = _merge(x2d, r, sbo, gates, wts["w_ret_o"], wts["w_sb_o"], wts["w_out"], wts["norm2_w"])
    y = _peer(h2t, x1, wts["w_pq_t"], wts["pk_keys"], wts["peer_u"], wts["peer_v_t"], wts["norm_f_w"])
    return (y.reshape(batch, seq, D_MODEL), sk.reshape(batch, seq, SB_HEADS, SB_D),
            sv.reshape(batch, seq, SB_HEADS, SB_D), s_fin)


def kernel(x_prompt, x_sample, cache_sb_k, cache_sb_v, state_ret, norm1_w, w_in, ret_gn_w, w_ret_o, w_sb_o,
           w_out, norm2_w, w_pq, pk_keys, peer_u, peer_v, norm_f_w):
    assert norm1_w.shape[0] == 1, "single-layer model"
    wts = dict(
        norm1_w=norm1_w[0], w_in=w_in[0].astype(BF16), ret_gn_w=ret_gn_w[0],
        w_ret_o=w_ret_o[0].astype(BF16), w_sb_o=w_sb_o[0].astype(BF16), w_out=w_out[0].astype(BF16),
        norm2_w=norm2_w[0], w_pq_t=w_pq[0].T.astype(BF16), pk_keys=pk_keys[0].astype(BF16),
        peer_u=peer_u[0].astype(BF16), peer_v_t=peer_v[0].T.astype(BF16), norm_f_w=norm_f_w)
    bp, tp, _ = x_prompt.shape
    bs, ts, _ = x_sample.shape
    past = cache_sb_k.shape[2]
    yp, kp, vp, sp = _layer(x_prompt, jnp.arange(tp, dtype=jnp.int32), None, None,
                            jnp.zeros((bp, RET_HEADS, RET_DK, RET_DV), F32), wts)
    ys, ks, vs, ss = _layer(x_sample, past + jnp.arange(ts, dtype=jnp.int32),
                            cache_sb_k[0].reshape(bs * past, SB_W), cache_sb_v[0].reshape(bs * past, SB_W),
                            state_ret[0], wts)
    return (yp, ys, kp[None], vp[None], sp[None], ks[None], vs[None], ss[None])
```

```python
import functools
import math

import numpy as np
import jax
import jax.numpy as jnp
from jax import lax
from jax.experimental import pallas as pl
from jax.experimental.pallas import tpu as pltpu

F32 = jnp.float32
BF16 = jnp.bfloat16

D_MODEL = 1024
CHUNK = 64
RET_HEADS = 4
RET_DK = 128
RET_DV = 256
SB_HEADS = 4
SB_D = 128
N_KEYS = 128
N_EXPERTS = N_KEYS * N_KEYS
PEER_HEADS = 8
PEER_HALF = 128
PEER_TOPK = 16
ROPE_BASE = 10000.0
EPS = 1e-6
RET_QK_W = RET_HEADS * RET_DK
RET_V_W = RET_HEADS * RET_DV
SB_W = SB_HEADS * SB_D
IN_W = 2 * RET_QK_W + 2 * RET_V_W + 3 * SB_W + 2 * D_MODEL

VMEM_LIMIT_BYTES = 56 * 1024 * 1024
NEG_BIG = -1e30


def _pick_tile(n, candidates):
    for c in candidates:
        if n % c == 0:
            return c
    raise ValueError(f"no tile in {candidates} divides {n}")


def _params(*sem):
    return pltpu.CompilerParams(dimension_semantics=sem, vmem_limit_bytes=VMEM_LIMIT_BYTES)


def _in_proj_kernel(x_ref, g_ref, w_ref, cos_ref, sin_ref,
                    rq_ref, rk_ref, rv_ref, rg_ref, sq_ref, sk_ref, sv_ref, skb_ref, svb_ref, gt_ref):
    x = x_ref[...]
    h = (x * lax.rsqrt(jnp.mean(x * x, axis=-1, keepdims=True) + EPS) * g_ref[...]).astype(BF16)

    def proj(lo, width):
        return jnp.dot(h, w_ref[:, lo:lo + width], preferred_element_type=F32)

    c = cos_ref[...]
    s = sin_ref[...]

    def rotary_store(p, out_ref, scale):
        for hh in range(RET_HEADS):
            ph = p[:, hh * RET_DK:(hh + 1) * RET_DK]
            r = ph * c + pltpu.roll(ph, RET_DK // 2, 1) * s
            if scale != 1.0:
                r = r * scale
            out_ref[:, hh * RET_DK:(hh + 1) * RET_DK] = r.astype(BF16)

    rotary_store(proj(0, RET_QK_W), rq_ref, 1.0)
    rotary_store(proj(RET_QK_W, RET_QK_W), rk_ref, RET_DK ** -0.5)
    off = 2 * RET_QK_W
    for j in range(RET_V_W // 512):
        rv_ref[:, j * 512:(j + 1) * 512] = proj(off + j * 512, 512).astype(BF16)
    off += RET_V_W
    for j in range(RET_V_W // 512):
        g = proj(off + j * 512, 512)
        rg_ref[:, j * 512:(j + 1) * 512] = (g / (1.0 + jnp.exp(-g))).astype(BF16)
    off += RET_V_W
    sq_ref[...] = (proj(off, SB_W) * SB_D ** -0.5).astype(BF16)
    off += SB_W
    k = proj(off, SB_W)
    sk_ref[...] = k
    skb_ref[...] = k.astype(BF16)
    off += SB_W
    v = proj(off, SB_W)
    sv_ref[...] = v
    svb_ref[...] = v.astype(BF16)
    off += SB_W
    for j in range(2 * D_MODEL // 512):
        g = proj(off + j * 512, 512)
        gt_ref[:, j * 512:(j + 1) * 512] = (1.0 / (1.0 + jnp.exp(-g))).astype(BF16)


def _in_proj(x2d, norm_w, w_in_bf, cos_tab, sin_tab):
    n = x2d.shape[0]
    tm = _pick_tile(n, (512, 256, 128, 64))
    assert cos_tab.shape[0] % tm == 0
    npos = cos_tab.shape[0] // tm
    row = lambda i: (i, 0)
    const = lambda i: (0, 0)
    pos = lambda i: (i % npos, 0)
    widths = [(RET_QK_W, BF16), (RET_QK_W, BF16), (RET_V_W, BF16), (RET_V_W, BF16), (SB_W, BF16),
              (SB_W, F32), (SB_W, F32), (SB_W, BF16), (SB_W, BF16), (2 * D_MODEL, BF16)]
    return pl.pallas_call(
        _in_proj_kernel,
        grid=(n // tm,),
        in_specs=[pl.BlockSpec((tm, D_MODEL), row),
                  pl.BlockSpec((1, D_MODEL), const),
                  pl.BlockSpec((D_MODEL, IN_W), const),
                  pl.BlockSpec((tm, RET_DK), pos),
                  pl.BlockSpec((tm, RET_DK), pos)],
        out_specs=[pl.BlockSpec((tm, w), row) for w, _ in widths],
        out_shape=[jax.ShapeDtypeStruct((n, w), dt) for w, dt in widths],
        compiler_params=_params("parallel"),
        name="in_proj",
    )(x2d, norm_w.reshape(1, D_MODEL), w_in_bf, cos_tab, sin_tab)


def _rotary_tables(pos):
    inv = ROPE_BASE ** (-jnp.arange(0, RET_DK, 2, dtype=F32) / RET_DK)
    ang = pos.astype(F32)[:, None] * inv[None, :]
    cos = jnp.cos(ang)
    sin = jnp.sin(ang)
    return jnp.concatenate([cos, cos], axis=1), jnp.concatenate([-sin, sin], axis=1)


def _retention_tables(blk):
    log_g = np.log(1.0 - 2.0 ** (-5.0 - np.arange(RET_HEADS, dtype=np.float64)))
    i = np.arange(blk, dtype=np.float64)
    dist = np.abs(i[:, None] - i[None, :])
    visible = (np.arange(blk)[None, :] // CHUNK) <= (np.arange(blk)[:, None] // CHUNK)
    dmat = np.where(visible[None], np.exp(dist[None] * log_g[:, None, None]), 0.0)
    inter = np.exp((i[None, :] + 1.0) * log_g[:, None])
    kdec = np.exp((blk - 1.0 - i)[None, :] * log_g[:, None])
    inter = np.broadcast_to(inter[:, :, None], (RET_HEADS, blk, RET_DV))
    kdec = np.broadcast_to(kdec[:, :, None], (RET_HEADS, blk, RET_DK))
    sdec = tuple(float(np.exp(blk * g)) for g in log_g)
    return (jnp.asarray(dmat, F32), jnp.asarray(inter, F32), jnp.asarray(kdec, F32)), sdec


def _retention_kernel(sdec, q_ref, k_ref, v_ref, gate_ref, s0_ref, dmat_ref, inter_ref, kdec_ref, gn_ref,
                      r_ref, sfin_ref, state_ref):
    t = pl.program_id(1)

    @pl.when(t == 0)
    def _():
        state_ref[...] = s0_ref[0]

    for hh in range(RET_HEADS):
        q = q_ref[:, hh * RET_DK:(hh + 1) * RET_DK]
        k = k_ref[:, hh * RET_DK:(hh + 1) * RET_DK]
        v = v_ref[:, hh * RET_DV:(hh + 1) * RET_DV]
        state = state_ref[hh]
        sc = lax.dot_general(q, k, (((1,), (1,)), ((), ())), preferred_element_type=F32)
        p = (sc * dmat_ref[hh]).astype(BF16)
        o = jnp.dot(p, v, preferred_element_type=F32)
        o = o + jnp.dot(q, state.astype(BF16), preferred_element_type=F32) * inter_ref[hh]
        kd = (k.astype(F32) * kdec_ref[hh]).astype(BF16)
        upd = lax.dot_general(kd, v, (((0,), (0,)), ((), ())), preferred_element_type=F32)
        state_ref[hh] = sdec[hh] * state + upd
        mu = jnp.mean(o, axis=-1, keepdims=True)
        cen = o - mu
        var = jnp.mean(cen * cen, axis=-1, keepdims=True)
        r = cen * lax.rsqrt(var + EPS) * gn_ref[:, hh * RET_DV:(hh + 1) * RET_DV]
        r = r * gate_ref[:, hh * RET_DV:(hh + 1) * RET_DV].astype(F32)
        r_ref[:, hh * RET_DV:(hh + 1) * RET_DV] = r.astype(BF16)

    @pl.when(t == pl.num_programs(1) - 1)
    def _():
        sfin_ref[0] = state_ref[...]


def _retention(rq, rk, rv, gate, s0, gn_w, batch, seq):
    blk = _pick_tile(seq, (256, 128, 64))
    nt = seq // blk
    tabs, sdec = _retention_tables(blk)
    row = lambda b, t: (b * nt + t, 0)
    per_b = lambda b, t: (b, 0, 0, 0)
    const3 = lambda b, t: (0, 0, 0)
    return pl.pallas_call(
        functools.partial(_retention_kernel, sdec),
        grid=(batch, nt),
        in_specs=[pl.BlockSpec((blk, RET_QK_W), row),
                  pl.BlockSpec((blk, RET_QK_W), row),
                  pl.BlockSpec((blk, RET_V_W), row),
                  pl.BlockSpec((blk, RET_V_W), row),
                  pl.BlockSpec((1, RET_HEADS, RET_DK, RET_DV), per_b),
                  pl.BlockSpec((RET_HEADS, blk, blk), const3),
                  pl.BlockSpec((RET_HEADS, blk, RET_DV), const3),
                  pl.BlockSpec((RET_HEADS, blk, RET_DK), const3),
                  pl.BlockSpec((1, RET_V_W), lambda b, t: (0, 0))],
        out_specs=[pl.BlockSpec((blk, RET_V_W), row),
                   pl.BlockSpec((1, RET_HEADS, RET_DK, RET_DV), per_b)],
        out_shape=[jax.ShapeDtypeStruct((batch * seq, RET_V_W), BF16),
                   jax.ShapeDtypeStruct((batch, RET_HEADS, RET_DK, RET_DV), F32)],
        scratch_shapes=[pltpu.VMEM((RET_HEADS, RET_DK, RET_DV), F32)],
        compiler_params=_params("parallel", "arbitrary"),
        name="retention",
    )(rq, rk, rv, gate, s0, *tabs, gn_w.reshape(1, RET_V_W))


SB_BLOCK = 256
SB_UNROLL = 4
SOFTPLUS_CLAMP = 30.0


def _suffix_matrix(n):
    return jnp.asarray(np.tril(np.ones((n, n), np.float32), -1), BF16)


def _sb_block(q, kblk, vblk, umat, carry, mask):
    z = lax.dot_general(q, kblk, (((1,), (1,)), ((), ())), preferred_element_type=F32)
    sp = jnp.maximum(jnp.log(1.0 + jnp.exp(jnp.minimum(z, SOFTPLUS_CLAMP))), z)
    spm = sp if mask is None else jnp.where(mask, sp, 0.0)
    after = jnp.dot(spm.astype(BF16), umat, preferred_element_type=F32)
    w = jnp.exp(z - sp - after - carry)
    if mask is not None:
        w = jnp.where(mask, w, 0.0)
    out = jnp.dot(w.astype(BF16), vblk, preferred_element_type=F32)
    return out, carry + after[:, 0:1] + spm[:, 0:1]


def _causal_mask(n):
    r = lax.broadcasted_iota(jnp.int32, (n, n), 0)
    c = lax.broadcasted_iota(jnp.int32, (n, n), 1)
    return c < r


def _sb_prompt_kernel(q_ref, k_ref, v_ref, u_ref, o_ref, acc_ref, carry_ref):
    qi = pl.program_id(2)
    blk = q_ref.shape[0]
    q = q_ref[...]
    umat = u_ref[...]
    start = pl.multiple_of(qi * blk, blk)
    out, carry = _sb_block(q, k_ref[pl.ds(start, blk), :], v_ref[pl.ds(start, blk), :], umat,
                           jnp.zeros((blk, 1), F32), _causal_mask(blk))
    acc_ref[...] = out
    carry_ref[...] = carry

    def wide_sweep(first):
        n = SB_UNROLL
        s = pl.multiple_of((first - n + 1) * blk, blk)
        keys = k_ref[pl.ds(s, n * blk), :]
        vals = v_ref[pl.ds(s, n * blk), :]
        z = lax.dot_general(q, keys, (((1,), (1,)), ((), ())), preferred_element_type=F32)
        sp = jnp.maximum(jnp.log(1.0 + jnp.exp(jnp.minimum(z, SOFTPLUS_CLAMP))), z)
        cols = [slice(u * blk, (u + 1) * blk) for u in range(n)]
        stacked = jnp.concatenate([sp[:, c] for c in cols], axis=0).astype(BF16)
        after = jnp.dot(stacked, umat, preferred_element_type=F32)
        carry = carry_ref[...]
        offsets = [None] * n
        for u in reversed(range(n)):
            offsets[u] = carry
            carry = carry + after[u * blk:(u + 1) * blk, 0:1] + sp[:, u * blk:u * blk + 1]
        later = jnp.concatenate([after[u * blk:(u + 1) * blk, :] + offsets[u] for u in range(n)], axis=1)
        w = jnp.exp(z - sp - later)
        acc_ref[...] += jnp.dot(w.astype(BF16), vals, preferred_element_type=F32)
        carry_ref[...] = carry

    def main_body(i, _):
        wide_sweep(qi - 1 - i * SB_UNROLL)
        return 0

    def tail_body(i, _):
        kj = qi % SB_UNROLL - 1 - i
        s = pl.multiple_of(kj * blk, blk)
        out, carry = _sb_block(q, k_ref[pl.ds(s, blk), :], v_ref[pl.ds(s, blk), :], umat, carry_ref[...], None)
        acc_ref[...] += out
        carry_ref[...] = carry
        return 0

    lax.fori_loop(0, qi // SB_UNROLL, main_body, 0)
    lax.fori_loop(0, qi % SB_UNROLL, tail_body, 0)
    o_ref[...] = acc_ref[...].astype(BF16)


def _sb_prompt(sq, skb, svb, batch, seq):
    blk = _pick_tile(seq, (SB_BLOCK, 128, 64))
    nq = seq // blk
    qmap = lambda b, h, i: (b * nq + i, h)
    kvmap = lambda b, h, i: (b, h)
    return pl.pallas_call(
        _sb_prompt_kernel,
        grid=(batch, SB_HEADS, nq),
        in_specs=[pl.BlockSpec((blk, SB_D), qmap),
                  pl.BlockSpec((seq, SB_D), kvmap),
                  pl.BlockSpec((seq, SB_D), kvmap),
                  pl.BlockSpec((blk, blk), lambda b, h, i: (0, 0))],
        out_specs=pl.BlockSpec((blk, SB_D), qmap),
        out_shape=jax.ShapeDtypeStruct((batch * seq, SB_W), BF16),
        scratch_shapes=[pltpu.VMEM((blk, SB_D), F32), pltpu.VMEM((blk, 1), F32)],
        compiler_params=_params("parallel", "parallel", "arbitrary"),
        name="sb_prompt",
    )(sq, skb, svb, _suffix_matrix(blk))


def _sb_sample_kernel(pblk, q_ref, k_ref, v_ref, pk_ref, pv_ref, u_ref, o_ref):
    t = q_ref.shape[0]
    past = pk_ref.shape[0]
    q = q_ref[...]
    acc, carry = _sb_block(q, k_ref[...], v_ref[...], u_ref[:t, :t], jnp.zeros((t, 1), F32), _causal_mask(t))
    for j in reversed(range(past // pblk)):
        kb = pk_ref[j * pblk:(j + 1) * pblk, :].astype(BF16)
        vb = pv_ref[j * pblk:(j + 1) * pblk, :].astype(BF16)
        out, carry = _sb_block(q, kb, vb, u_ref[:pblk, :pblk], carry, None)
        acc = acc + out
    o_ref[...] = acc.astype(BF16)


def _sb_sample(sq, skb, svb, past_k, past_v, batch, seq, past):
    pblk = _pick_tile(past, (SB_BLOCK, 128, 64))
    ublk = max(pblk, seq)
    cur = lambda b, h: (b, h)
    return pl.pallas_call(
        functools.partial(_sb_sample_kernel, pblk),
        grid=(batch, SB_HEADS),
        in_specs=[pl.BlockSpec((seq, SB_D), cur),
                  pl.BlockSpec((seq, SB_D), cur),
                  pl.BlockSpec((seq, SB_D), cur),
                  pl.BlockSpec((past, SB_D), cur),
                  pl.BlockSpec((past, SB_D), cur),
                  pl.BlockSpec((ublk, ublk), lambda b, h: (0, 0))],
        out_specs=pl.BlockSpec((seq, SB_D), cur),
        out_shape=jax.ShapeDtypeStruct((batch * seq, SB_W), BF16),
        compiler_params=_params("parallel", "parallel"),
        name="sb_sample",
    )(sq, skb, svb, past_k, past_v, _suffix_matrix(ublk))


def _merge_kernel(x_ref, r_ref, sb_ref, gt_ref, wr_ref, ws_ref, wo_ref, g2_ref, x1_ref, h2t_ref):
    rb = jnp.dot(r_ref[...], wr_ref[...], preferred_element_type=F32)
    sb = jnp.dot(sb_ref[...], ws_ref[...], preferred_element_type=F32)
    mixed = gt_ref[:, :D_MODEL].astype(F32) * rb + gt_ref[:, D_MODEL:].astype(F32) * sb
    x1 = x_ref[...] + jnp.dot(mixed.astype(BF16), wo_ref[...], preferred_element_type=F32)
    x1_ref[...] = x1
    h2 = x1 * lax.rsqrt(jnp.mean(x1 * x1, axis=-1, keepdims=True) + EPS) * g2_ref[...]
    h2t_ref[...] = h2.T.astype(BF16)


def _merge(x2d, r, sbo, gates, w_ret_o, w_sb_o, w_out, norm2_w):
    n = x2d.shape[0]
    tm = _pick_tile(n, (512, 256, 128))
    row = lambda i: (i, 0)
    const = lambda i: (0, 0)
    return pl.pallas_call(
        _merge_kernel,
        grid=(n // tm,),
        in_specs=[pl.BlockSpec((tm, D_MODEL), row),
                  pl.BlockSpec((tm, RET_V_W), row),
                  pl.BlockSpec((tm, SB_W), row),
                  pl.BlockSpec((tm, 2 * D_MODEL), row),
                  pl.BlockSpec((RET_V_W, D_MODEL), const),
                  pl.BlockSpec((SB_W, D_MODEL), const),
                  pl.BlockSpec((D_MODEL, D_MODEL), const),
                  pl.BlockSpec((1, D_MODEL), const)],
        out_specs=[pl.BlockSpec((tm, D_MODEL), row),
                   pl.BlockSpec((D_MODEL, tm), lambda i: (0, i))],
        out_shape=[jax.ShapeDtypeStruct((n, D_MODEL), F32),
                   jax.ShapeDtypeStruct((D_MODEL, n), BF16)],
        compiler_params=_params("parallel"),
        name="merge",
    )(x2d, r, sbo, gates, w_ret_o, w_sb_o, w_out, norm2_w.reshape(1, D_MODEL))


PEER_SLAB = 1024


def _sort_desc(a):
    a = list(a)
    n = len(a)
    k = 2
    while k <= n:
        j = k // 2
        while j >= 1:
            for i in range(n):
                l = i ^ j
                if l > i:
                    hi = jnp.maximum(a[i], a[l])
                    lo = jnp.minimum(a[i], a[l])
                    a[i], a[l] = (hi, lo) if (i & k) == 0 else (lo, hi)
            j //= 2
        k *= 2
    return a


def _merge_bitonic_desc(a):
    a = list(a)
    n = len(a)
    j = n // 2
    while j >= 1:
        for i in range(n):
            l = i ^ j
            if l > i:
                a[i], a[l] = jnp.maximum(a[i], a[l]), jnp.minimum(a[i], a[l])
        j //= 2
    return a


def _top_merge(a, b):
    n = len(a)
    return _merge_bitonic_desc([jnp.maximum(a[i], b[n - 1 - i]) for i in range(n)])


def _top16_rows(s):
    groups = s.shape[0] // 8
    a = _sort_desc([s[8 * j:8 * j + 8, :] for j in range(groups)])
    for shift in (4, 2, 1):
        a = _top_merge(a, [pltpu.roll(x, shift, 0) for x in a])
    return a


def _twin_bf16_words(x):
    hi = pltpu.bitcast(x.astype(BF16).astype(F32), jnp.uint32)
    return hi | (hi >> 16)


def _lookup_desc(keys, vals, default, s):
    rows, tn = s.shape
    s3 = s.reshape(rows // 8, 8, tn)
    out = jnp.full_like(s3, default)
    for b in reversed(range(len(keys))):
        v = vals[b] if isinstance(vals[b], float) else vals[b][None]
        out = jnp.where(s3 == keys[b][None], v, out)
    return out.reshape(rows, tn)


def _peer_kernel(n_slabs, h2t_ref, x1_ref, wq_ref, keys_ref, u_ref, vt_ref, gf_ref, y_ref,
                 s1_ref, s2_ref, cnt_ref, e1_ref, rank2_ref, e2_ref, coef0_ref, coef1_ref, gate0_ref, gate1_ref,
                 acc_ref):
    j = pl.program_id(1)
    ns = n_slabs
    coef_refs = (coef0_ref, coef1_ref)
    gate_refs = (gate0_ref, gate1_ref)
    tn = h2t_ref.shape[1]
    ht = h2t_ref[...]

    @pl.when(j == 0)
    def _():
        qt = jnp.dot(wq_ref[...], ht, preferred_element_type=F32).astype(BF16)
        half_rows = PEER_HEADS * PEER_HALF
        for p, dst in ((0, s1_ref), (1, s2_ref)):
            s = jnp.dot(keys_ref[p], qt[p * half_rows:(p + 1) * half_rows, :], preferred_element_type=F32)
            dst[...] = s.reshape(PEER_HEADS, N_KEYS, tn)

        lw = 128

        def selection_tables(ci, _):
            lanes = pl.ds(pl.multiple_of(ci * lw, lw), lw)
            sub = lax.broadcasted_iota(jnp.int32, (8, lw), 0)
            v1 = [jnp.zeros((8, lw), F32) for _ in range(PEER_TOPK)]
            v2 = [jnp.zeros((8, lw), F32) for _ in range(PEER_TOPK)]
            ranks = [float(b) for b in range(PEER_TOPK)]
            for hh in range(PEER_HEADS):
                top = _top16_rows(s1_ref[hh, :, lanes])
                v1 = [jnp.where(sub == hh, top[a], v1[a]) for a in range(PEER_TOPK)]
                s2 = s2_ref[hh, :, lanes]
                top = _top16_rows(s2)
                v2 = [jnp.where(sub == hh, top[a], v2[a]) for a in range(PEER_TOPK)]
                rank2_ref[hh, :, lanes] = _lookup_desc(top, ranks, float(PEER_TOPK), s2).astype(BF16)
                e2 = jnp.exp(s2.reshape(N_KEYS // 8, 8, lw) - top[0][None])
                e2_ref[hh, :, lanes] = e2.reshape(N_KEYS, lw).astype(BF16)
            cand_rows = [[v1[a] + v2[b] for b in range(PEER_TOPK // (a + 1))] for a in range(PEER_TOPK)]
            cands = [c for row in cand_rows for c in row]
            pad = [jnp.full((8, lw), NEG_BIG, F32)] * (-len(cands) % PEER_TOPK)
            padded = cands + pad
            lists = [_sort_desc(padded[i:i + PEER_TOPK]) for i in range(0, len(padded), PEER_TOPK)]
            while len(lists) > 1:
                lists = [_top_merge(lists[i], lists[i + 1]) for i in range(0, len(lists), 2)]
            tau = lists[0][PEER_TOPK - 1]
            m = cands[0]
            z = jnp.zeros((8, lw), F32)
            for cnd in cands:
                z = z + jnp.where(cnd >= tau, jnp.exp(cnd - m), 0.0)
            shift = v1[0] + jnp.log(z)
            cnt_rows = []
            for row in cand_rows:
                n_sel = jnp.zeros((8, lw), F32)
                for cnd in row:
                    n_sel = n_sel + jnp.where(cnd >= tau, 1.0, 0.0)
                cnt_rows.append(n_sel)
            for hh in range(PEER_HEADS):
                s1 = s1_ref[hh, :, lanes]
                head_row = lambda x: jnp.broadcast_to(x[hh:hh + 1, :], (8, lw))
                cnt = _lookup_desc([head_row(v) for v in v1], [head_row(c) for c in cnt_rows], 0.0, s1)
                cnt_ref[hh, :, lanes] = _twin_bf16_words(cnt)
                e1_ref[hh, :, lanes] = _twin_bf16_words(jnp.exp(s1 - shift[hh:hh + 1, :]))
            return 0

        lax.fori_loop(0, tn // lw, selection_tables, 0)
        acc_ref[...] = jnp.zeros_like(acc_ref)
        coef_refs[1][...] = jnp.zeros_like(coef_refs[1])

    n_i1 = u_ref.shape[0] // N_KEYS
    pack = 16

    def gate_rows(slab):
        out = []
        for ii in range(n_i1):
            i1 = jnp.minimum(slab * n_i1 + ii, N_KEYS - 1)
            out.append([(cnt_ref[hh, pl.ds(i1, 1), :], e1_ref[hh, pl.ds(i1, 1), :])
                        for hh in range(PEER_HEADS)])
        return out

    def gate_group(dst_ref, table_rows, ii):
        as_tile = lambda row: pltpu.bitcast(jnp.broadcast_to(row, (pack // 2, tn)), BF16)
        cnt_b = [as_tile(c) for c, _ in table_rows[ii]]
        e1_b = [as_tile(e) for _, e in table_rows[ii]]
        for c in range(N_KEYS // pack):
            rows = slice(c * pack, (c + 1) * pack)
            g = jnp.zeros((pack, tn), BF16)
            for hh in range(PEER_HEADS):
                w = e2_ref[hh, rows, :] * e1_b[hh]
                g = g + jnp.where(rank2_ref[hh, rows, :] < cnt_b[hh], w, jnp.zeros_like(w))
            r0 = ii * N_KEYS + c * pack
            dst_ref[r0:r0 + pack, :] = g

    @pl.when(j == 0)
    def _():
        rows0 = gate_rows(0)
        for ii in range(n_i1):
            gate_group(gate_refs[0], rows0, ii)

    def add_previous_slab(prev_ref):
        acc_ref[...] += jnp.dot(vt_ref[...], prev_ref[...], preferred_element_type=F32)

    def build_slab(parity):
        cur_ref, prev_ref = coef_refs[parity], coef_refs[1 - parity]
        gate_ref, next_gate_ref = gate_refs[parity], gate_refs[1 - parity]
        next_rows = gate_rows(j + 1)
        n_split = 2 if tn % 512 == 0 else 1
        wn = tn // n_split
        m_split = 2
        em = u_ref.shape[0] // m_split
        dm = D_MODEL // m_split
        gm = em // N_KEYS
        assert n_i1 == m_split * gm

        dots = {}

        def expert_dot_quadrant(mi, ni):
            dots[mi, ni] = jnp.dot(u_ref[mi * em:(mi + 1) * em, :], h2t_ref[:, ni * wn:(ni + 1) * wn],
                                   preferred_element_type=F32)

        def previous_slab_quadrant(mi, ni):
            blk = (slice(mi * dm, (mi + 1) * dm), slice(ni * wn, (ni + 1) * wn))
            acc_ref[blk] += jnp.dot(vt_ref[mi * dm:(mi + 1) * dm, :], prev_ref[:, ni * wn:(ni + 1) * wn],
                                    preferred_element_type=F32)

        def coef_group(ii):
            mi, local = divmod(ii, gm)
            for c in range(N_KEYS // pack):
                for ni in range(n_split):
                    lr = local * N_KEYS + c * pack
                    act = jax.nn.gelu(dots[mi, ni][lr:lr + pack, :].astype(BF16))
                    r0 = ii * N_KEYS + c * pack
                    cols = slice(ni * wn, (ni + 1) * wn)
                    cur_ref[r0:r0 + pack, cols] = gate_ref[r0:r0 + pack, cols] * act

        quads = [(mi, ni) for mi in range(m_split) for ni in range(n_split)]
        matmuls = ([functools.partial(expert_dot_quadrant, *q) for q in quads]
                   + [functools.partial(previous_slab_quadrant, *q) for q in quads])
        tasks = [[] for _ in matmuls]
        first_coef_slot = n_split
        for ii in range(n_i1):
            slot = min(first_coef_slot + ii // 2, len(tasks) - 1)
            slot = max(slot, (ii // gm + 1) * n_split)
            tasks[min(slot, len(tasks) - 1)].append(functools.partial(coef_group, ii))
        for ii in range(n_i1):
            tasks[ii * len(tasks) // n_i1].append(functools.partial(gate_group, next_gate_ref, next_rows, ii))
        for mm, work in zip(matmuls, tasks):
            mm()
            for w in work:
                w()

    for parity in range(2):
        @pl.when(jnp.logical_and(j < ns, j % 2 == parity))
        def _():
            build_slab(parity)

    @pl.when(j == ns)
    def _():
        add_previous_slab(coef_refs[(n_slabs - 1) % 2])
        x2 = x1_ref[...] + acc_ref[...].T
        y = x2 * lax.rsqrt(jnp.mean(x2 * x2, axis=-1, keepdims=True) + EPS) * gf_ref[...]
        y_ref[...] = y


def _peer(h2t, x1, wq_t, keys_bf, u_bf, vt_bf, norm_f_w):
    n = x1.shape[0]
    tn = _pick_tile(n, (512, 256, 128))
    ns = N_EXPERTS // PEER_SLAB
    table = (PEER_HEADS, N_KEYS, tn)
    return pl.pallas_call(
        functools.partial(_peer_kernel, ns),
        grid=(n // tn, ns + 1),
        in_specs=[pl.BlockSpec((D_MODEL, tn), lambda i, j: (0, i)),
                  pl.BlockSpec((tn, D_MODEL), lambda i, j: (i, 0)),
                  pl.BlockSpec((2 * PEER_HEADS * PEER_HALF, D_MODEL), lambda i, j: (0, 0)),
                  pl.BlockSpec((2, PEER_HEADS * N_KEYS, PEER_HEADS * PEER_HALF), lambda i, j: (0, 0, 0)),
                  pl.BlockSpec((PEER_SLAB, D_MODEL), lambda i, j: (jnp.minimum(j, ns - 1), 0)),
                  pl.BlockSpec((D_MODEL, PEER_SLAB), lambda i, j: (0, jnp.maximum(j - 1, 0))),
                  pl.BlockSpec((1, D_MODEL), lambda i, j: (0, 0))],
        out_specs=pl.BlockSpec((tn, D_MODEL), lambda i, j: (i, 0)),
        out_shape=jax.ShapeDtypeStruct((n, D_MODEL), F32),
        scratch_shapes=[pltpu.VMEM(table, F32),
                        pltpu.VMEM(table, F32),
                        pltpu.VMEM(table, jnp.uint32),
                        pltpu.VMEM(table, jnp.uint32),
                        pltpu.VMEM(table, BF16),
                        pltpu.VMEM(table, BF16),
                        pltpu.VMEM((PEER_SLAB, tn), BF16),
                        pltpu.VMEM((PEER_SLAB, tn), BF16),
                        pltpu.VMEM((PEER_SLAB, tn), BF16),
                        pltpu.VMEM((PEER_SLAB, tn), BF16),
                        pltpu.VMEM((D_MODEL, tn), F32)],
        compiler_params=_params("parallel", "arbitrary"),
        name="peer",
    )(h2t, x1, wq_t, keys_bf, u_bf, vt_bf, norm_f_w.reshape(1, D_MODEL))


def _layer(x, pos, past_k, past_v, s0, wts):
    batch, seq, _ = x.shape
    n = batch * seq
    x2d = x.reshape(n, D_MODEL)
    cos_tab, sin_tab = _rotary_tables(pos)
    tm = _pick_tile(n, (512, 256, 128, 64))
    if seq < tm:
        cos_tab = jnp.tile(cos_tab, (tm // seq, 1))
        sin_tab = jnp.tile(sin_tab, (tm // seq, 1))
    rq, rk, rv, gate, sq, sk, sv, skb, svb, gates = _in_proj(x2d, wts["norm1_w"], wts["w_in"], cos_tab, sin_tab)
    r, s_fin = _retention(rq, rk, rv, gate, s0, wts["ret_gn_w"], batch, seq)
    if past_k is None:
        sbo = _sb_prompt(sq, skb, svb, batch, seq)
    else:
        sbo = _sb_sample(sq, skb, svb, past_k, past_v, batch, seq, past_k.shape[0] // batch)
    x1, h2t = _merge(x2d, r, sbo, gates, wts["w_ret_o"], wts["w_sb_o"], wts["w_out"], wts["norm2_w"])
    y = _peer(h2t, x1, wts["w_pq_t"], wts["pk_keys"], wts["peer_u"], wts["peer_v_t"], wts["norm_f_w"])
    return (y.reshape(batch, seq, D_MODEL), sk.reshape(batch, seq, SB_HEADS, SB_D),
            sv.reshape(batch, seq, SB_HEADS, SB_D), s_fin)


def kernel(x_prompt, x_sample, cache_sb_k, cache_sb_v, state_ret, norm1_w, w_in, ret_gn_w, w_ret_o, w_sb_o,
           w_out, norm2_w, w_pq, pk_keys, peer_u, peer_v, norm_f_w):
    assert norm1_w.shape[0] == 1, "single-layer model"
    wq_t = w_pq[0].reshape(D_MODEL, PEER_HEADS, 2, PEER_HALF).transpose(2, 1, 3, 0)
    wq_t = wq_t.reshape(2 * PEER_HEADS * PEER_HALF, D_MODEL).astype(BF16)
    eye = jnp.eye(PEER_HEADS, dtype=pk_keys.dtype)
    keys_bd = pk_keys[0][:, :, :, None, :] * eye[None, :, None, :, None]
    keys_bd = keys_bd.reshape(2, PEER_HEADS * N_KEYS, PEER_HEADS * PEER_HALF).astype(BF16)
    wts = dict(
        norm1_w=norm1_w[0], w_in=w_in[0].astype(BF16), ret_gn_w=ret_gn_w[0],
        w_ret_o=w_ret_o[0].astype(BF16), w_sb_o=w_sb_o[0].astype(BF16), w_out=w_out[0].astype(BF16),
        norm2_w=norm2_w[0], w_pq_t=wq_t, pk_keys=keys_bd,
        peer_u=peer_u[0].astype(BF16), peer_v_t=peer_v[0].T.astype(BF16), norm_f_w=norm_f_w)
    bp, tp, _ = x_prompt.shape
    bs, ts, _ = x_sample.shape
    past = cache_sb_k.shape[2]
    yp, kp, vp, sp = _layer(x_prompt, jnp.arange(tp, dtype=jnp.int32), None, None,
                            jnp.zeros((bp, RET_HEADS, RET_DK, RET_DV), F32), wts)
    ys, ks, vs, ss = _layer(x_sample, past + jnp.arange(ts, dtype=jnp.int32),
                            cache_sb_k[0].reshape(bs * past, SB_W), cache_sb_v[0].reshape(bs * past, SB_W),
                            state_ret[0], wts)
    return (yp, ys, kp[None], vp[None], sp[None], ks[None], vs[None], ss[None])
```

```python
import functools
import math

import numpy as np
import jax
import jax.numpy as jnp
from jax import lax
from jax.experimental import pallas as pl
from jax.experimental.pallas import tpu as pltpu

F32 = jnp.float32
BF16 = jnp.bfloat16

D_MODEL = 1024
CHUNK = 64
RET_HEADS = 4
RET_DK = 128
RET_DV = 256
SB_HEADS = 4
SB_D = 128
N_KEYS = 128
N_EXPERTS = N_KEYS * N_KEYS
PEER_HEADS = 8
PEER_HALF = 128
PEER_TOPK = 16
ROPE_BASE = 10000.0
EPS = 1e-6
RET_QK_W = RET_HEADS * RET_DK
RET_V_W = RET_HEADS * RET_DV
SB_W = SB_HEADS * SB_D
IN_W = 2 * RET_QK_W + 2 * RET_V_W + 3 * SB_W + 2 * D_MODEL

VMEM_LIMIT_BYTES = 56 * 1024 * 1024
NEG_BIG = -1e30


def _pick_tile(n, candidates):
    for c in candidates:
        if n % c == 0:
            return c
    raise ValueError(f"no tile in {candidates} divides {n}")


def _params(*sem):
    return pltpu.CompilerParams(dimension_semantics=sem, vmem_limit_bytes=VMEM_LIMIT_BYTES)


def _in_proj_kernel(x_ref, g_ref, w_ref, cos_ref, sin_ref,
                    rq_ref, rk_ref, rv_ref, rg_ref, sq_ref, sk_ref, sv_ref, skb_ref, svb_ref, gt_ref):
    x = x_ref[...]
    h = (x * lax.rsqrt(jnp.mean(x * x, axis=-1, keepdims=True) + EPS) * g_ref[...]).astype(BF16)

    def proj(lo, width):
        return jnp.dot(h, w_ref[:, lo:lo + width], preferred_element_type=F32)

    c = cos_ref[...]
    s = sin_ref[...]

    def rotary_store(p, out_ref, scale):
        for hh in range(RET_HEADS):
            ph = p[:, hh * RET_DK:(hh + 1) * RET_DK]
            r = ph * c + pltpu.roll(ph, RET_DK // 2, 1) * s
            if scale != 1.0:
                r = r * scale
            out_ref[:, hh * RET_DK:(hh + 1) * RET_DK] = r.astype(BF16)

    rotary_store(proj(0, RET_QK_W), rq_ref, 1.0)
    rotary_store(proj(RET_QK_W, RET_QK_W), rk_ref, RET_DK ** -0.5)
    off = 2 * RET_QK_W
    for j in range(RET_V_W // 512):
        rv_ref[:, j * 512:(j + 1) * 512] = proj(off + j * 512, 512).astype(BF16)
    off += RET_V_W
    for j in range(RET_V_W // 512):
        g = proj(off + j * 512, 512)
        rg_ref[:, j * 512:(j + 1) * 512] = (g / (1.0 + jnp.exp(-g))).astype(BF16)
    off += RET_V_W
    sq_ref[...] = (proj(off, SB_W) * SB_D ** -0.5).astype(BF16)
    off += SB_W
    tm = x_ref.shape[0]

    def store_rows_by_head(out_ref, val):
        for hh in range(SB_HEADS):
            out_ref[pl.ds(hh, tm, stride=SB_HEADS), :] = val[:, hh * SB_D:(hh + 1) * SB_D]

    k = proj(off, SB_W)
    store_rows_by_head(sk_ref, k)
    skb_ref[...] = k.astype(BF16)
    off += SB_W
    v = proj(off, SB_W)
    store_rows_by_head(sv_ref, v)
    svb_ref[...] = v.astype(BF16)
    off += SB_W
    for j in range(2 * D_MODEL // 512):
        g = proj(off + j * 512, 512)
        gt_ref[:, j * 512:(j + 1) * 512] = (1.0 / (1.0 + jnp.exp(-g))).astype(BF16)


def _in_proj(x2d, norm_w, w_in_bf, cos_tab, sin_tab):
    n = x2d.shape[0]
    tm = _pick_tile(n, (512, 256, 128, 64))
    assert cos_tab.shape[0] % tm == 0
    npos = cos_tab.shape[0] // tm
    row = lambda i: (i, 0)
    const = lambda i: (0, 0)
    pos = lambda i: (i % npos, 0)
    outs = [(1, RET_QK_W, BF16), (1, RET_QK_W, BF16), (1, RET_V_W, BF16), (1, RET_V_W, BF16), (1, SB_W, BF16),
            (SB_HEADS, SB_D, F32), (SB_HEADS, SB_D, F32), (1, SB_W, BF16), (1, SB_W, BF16),
            (1, 2 * D_MODEL, BF16)]
    return pl.pallas_call(
        _in_proj_kernel,
        grid=(n // tm,),
        in_specs=[pl.BlockSpec((tm, D_MODEL), row),
                  pl.BlockSpec((1, D_MODEL), const),
                  pl.BlockSpec((D_MODEL, IN_W), const),
                  pl.BlockSpec((tm, RET_DK), pos),
                  pl.BlockSpec((tm, RET_DK), pos)],
        out_specs=[pl.BlockSpec((tm * r, w), row) for r, w, _ in outs],
        out_shape=[jax.ShapeDtypeStruct((n * r, w), dt) for r, w, dt in outs],
        compiler_params=_params("parallel"),
        name="in_proj",
    )(x2d, norm_w.reshape(1, D_MODEL), w_in_bf, cos_tab, sin_tab)


def _rotary_tables(pos):
    inv = ROPE_BASE ** (-jnp.arange(0, RET_DK, 2, dtype=F32) / RET_DK)
    ang = pos.astype(F32)[:, None] * inv[None, :]
    cos = jnp.cos(ang)
    sin = jnp.sin(ang)
    return jnp.concatenate([cos, cos], axis=1), jnp.concatenate([-sin, sin], axis=1)


def _retention_tables(blk):
    log_g = np.log(1.0 - 2.0 ** (-5.0 - np.arange(RET_HEADS, dtype=np.float64)))
    i = np.arange(blk, dtype=np.float64)
    dist = np.abs(i[:, None] - i[None, :])
    visible = (np.arange(blk)[None, :] // CHUNK) <= (np.arange(blk)[:, None] // CHUNK)
    dmat = np.where(visible[None], np.exp(dist[None] * log_g[:, None, None]), 0.0)
    inter = np.exp((i[None, :] + 1.0) * log_g[:, None])
    kdec = np.exp((blk - 1.0 - i)[None, :] * log_g[:, None])
    inter = np.broadcast_to(inter[:, :, None], (RET_HEADS, blk, RET_DV))
    kdec = np.broadcast_to(kdec[:, :, None], (RET_HEADS, blk, RET_DK))
    sdec = tuple(float(np.exp(blk * g)) for g in log_g)
    return (jnp.asarray(dmat, F32), jnp.asarray(inter, F32), jnp.asarray(kdec, F32)), sdec


def _retention_kernel(sdec, q_ref, k_ref, v_ref, gate_ref, s0_ref, dmat_ref, inter_ref, kdec_ref, gn_ref,
                      r_ref, sfin_ref, state_ref):
    t = pl.program_id(1)

    @pl.when(t == 0)
    def _():
        state_ref[...] = s0_ref[0]

    for hh in range(RET_HEADS):
        q = q_ref[:, hh * RET_DK:(hh + 1) * RET_DK]
        k = k_ref[:, hh * RET_DK:(hh + 1) * RET_DK]
        v = v_ref[:, hh * RET_DV:(hh + 1) * RET_DV]
        state = state_ref[hh]
        sc = lax.dot_general(q, k, (((1,), (1,)), ((), ())), preferred_element_type=F32)
        p = (sc * dmat_ref[hh]).astype(BF16)
        o = jnp.dot(p, v, preferred_element_type=F32)
        o = o + jnp.dot(q, state.astype(BF16), preferred_element_type=F32) * inter_ref[hh]
        kd = (k.astype(F32) * kdec_ref[hh]).astype(BF16)
        upd = lax.dot_general(kd, v, (((0,), (0,)), ((), ())), preferred_element_type=F32)
        state_ref[hh] = sdec[hh] * state + upd
        mu = jnp.mean(o, axis=-1, keepdims=True)
        cen = o - mu
        var = jnp.mean(cen * cen, axis=-1, keepdims=True)
        r = cen * lax.rsqrt(var + EPS) * gn_ref[:, hh * RET_DV:(hh + 1) * RET_DV]
        r = r * gate_ref[:, hh * RET_DV:(hh + 1) * RET_DV].astype(F32)
        r_ref[:, hh * RET_DV:(hh + 1) * RET_DV] = r.astype(BF16)

    @pl.when(t == pl.num_programs(1) - 1)
    def _():
        sfin_ref[0] = state_ref[...]


def _retention(rq, rk, rv, gate, s0, gn_w, batch, seq):
    blk = _pick_tile(seq, (256, 128, 64))
    nt = seq // blk
    tabs, sdec = _retention_tables(blk)
    row = lambda b, t: (b * nt + t, 0)
    per_b = lambda b, t: (b, 0, 0, 0)
    const3 = lambda b, t: (0, 0, 0)
    return pl.pallas_call(
        functools.partial(_retention_kernel, sdec),
        grid=(batch, nt),
        in_specs=[pl.BlockSpec((blk, RET_QK_W), row),
                  pl.BlockSpec((blk, RET_QK_W), row),
                  pl.BlockSpec((blk, RET_V_W), row),
                  pl.BlockSpec((blk, RET_V_W), row),
                  pl.BlockSpec((1, RET_HEADS, RET_DK, RET_DV), per_b),
                  pl.BlockSpec((RET_HEADS, blk, blk), const3),
                  pl.BlockSpec((RET_HEADS, blk, RET_DV), const3),
                  pl.BlockSpec((RET_HEADS, blk, RET_DK), const3),
                  pl.BlockSpec((1, RET_V_W), lambda b, t: (0, 0))],
        out_specs=[pl.BlockSpec((blk, RET_V_W), row),
                   pl.BlockSpec((1, RET_HEADS, RET_DK, RET_DV), per_b)],
        out_shape=[jax.ShapeDtypeStruct((batch * seq, RET_V_W), BF16),
                   jax.ShapeDtypeStruct((batch, RET_HEADS, RET_DK, RET_DV), F32)],
        scratch_shapes=[pltpu.VMEM((RET_HEADS, RET_DK, RET_DV), F32)],
        compiler_params=_params("parallel", "arbitrary"),
        name="retention",
    )(rq, rk, rv, gate, s0, *tabs, gn_w.reshape(1, RET_V_W))


SB_BLOCK = 256
SB_UNROLL = 4
SOFTPLUS_CLAMP = 30.0


def _suffix_matrix(n):
    return jnp.asarray(np.tril(np.ones((n, n), np.float32), -1), BF16)


def _sb_block(q, kblk, vblk, umat, carry, mask):
    z = lax.dot_general(q, kblk, (((1,), (1,)), ((), ())), preferred_element_type=F32)
    sp = jnp.maximum(jnp.log(1.0 + jnp.exp(jnp.minimum(z, SOFTPLUS_CLAMP))), z)
    spm = sp if mask is None else jnp.where(mask, sp, 0.0)
    after = jnp.dot(spm.astype(BF16), umat, preferred_element_type=F32)
    w = jnp.exp(z - sp - after - carry)
    if mask is not None:
        w = jnp.where(mask, w, 0.0)
    out = jnp.dot(w.astype(BF16), vblk, preferred_element_type=F32)
    return out, carry + after[:, 0:1] + spm[:, 0:1]


def _causal_mask(n):
    r = lax.broadcasted_iota(jnp.int32, (n, n), 0)
    c = lax.broadcasted_iota(jnp.int32, (n, n), 1)
    return c < r


def _sb_prompt_kernel(q_ref, k_ref, v_ref, u_ref, o_ref, acc_ref, carry_ref):
    qi = pl.program_id(2)
    blk = q_ref.shape[0]
    q = q_ref[...]
    umat = u_ref[...]
    start = pl.multiple_of(qi * blk, blk)
    out, carry = _sb_block(q, k_ref[pl.ds(start, blk), :], v_ref[pl.ds(start, blk), :], umat,
                           jnp.zeros((blk, 1), F32), _causal_mask(blk))
    acc_ref[...] = out
    carry_ref[...] = carry

    def wide_sweep(first):
        n = SB_UNROLL
        s = pl.multiple_of((first - n + 1) * blk, blk)
        keys = k_ref[pl.ds(s, n * blk), :]
        vals = v_ref[pl.ds(s, n * blk), :]
        z = lax.dot_general(q, keys, (((1,), (1,)), ((), ())), preferred_element_type=F32)
        sp = jnp.maximum(jnp.log(1.0 + jnp.exp(jnp.minimum(z, SOFTPLUS_CLAMP))), z)
        cols = [slice(u * blk, (u + 1) * blk) for u in range(n)]
        stacked = jnp.concatenate([sp[:, c] for c in cols], axis=0).astype(BF16)
        after = jnp.dot(stacked, umat, preferred_element_type=F32)
        carry = carry_ref[...]
        offsets = [None] * n
        for u in reversed(range(n)):
            offsets[u] = carry
            carry = carry + after[u * blk:(u + 1) * blk, 0:1] + sp[:, u * blk:u * blk + 1]
        later = jnp.concatenate([after[u * blk:(u + 1) * blk, :] + offsets[u] for u in range(n)], axis=1)
        w = jnp.exp(z - sp - later)
        acc_ref[...] += jnp.dot(w.astype(BF16), vals, preferred_element_type=F32)
        carry_ref[...] = carry

    def main_body(i, _):
        wide_sweep(qi - 1 - i * SB_UNROLL)
        return 0

    def tail_body(i, _):
        kj = qi % SB_UNROLL - 1 - i
        s = pl.multiple_of(kj * blk, blk)
        out, carry = _sb_block(q, k_ref[pl.ds(s, blk), :], v_ref[pl.ds(s, blk), :], umat, carry_ref[...], None)
        acc_ref[...] += out
        carry_ref[...] = carry
        return 0

    lax.fori_loop(0, qi // SB_UNROLL, main_body, 0)
    lax.fori_loop(0, qi % SB_UNROLL, tail_body, 0)
    o_ref[...] = acc_ref[...].astype(BF16)


def _sb_prompt(sq, skb, svb, batch, seq):
    blk = _pick_tile(seq, (SB_BLOCK, 128, 64))
    nq = seq // blk
    qmap = lambda b, h, i: (b * nq + i, h)
    kvmap = lambda b, h, i: (b, h)
    return pl.pallas_call(
        _sb_prompt_kernel,
        grid=(batch, SB_HEADS, nq),
        in_specs=[pl.BlockSpec((blk, SB_D), qmap),
                  pl.BlockSpec((seq, SB_D), kvmap),
                  pl.BlockSpec((seq, SB_D), kvmap),
                  pl.BlockSpec((blk, blk), lambda b, h, i: (0, 0))],
        out_specs=pl.BlockSpec((blk, SB_D), qmap),
        out_shape=jax.ShapeDtypeStruct((batch * seq, SB_W), BF16),
        scratch_shapes=[pltpu.VMEM((blk, SB_D), F32), pltpu.VMEM((blk, 1), F32)],
        compiler_params=_params("parallel", "parallel", "arbitrary"),
        name="sb_prompt",
    )(sq, skb, svb, _suffix_matrix(blk))


def _sb_sample_kernel(pblk, q_ref, k_ref, v_ref, pk_ref, pv_ref, u_ref, o_ref):
    t = q_ref.shape[0]
    past = pk_ref.shape[0]
    q = q_ref[...]
    acc, carry = _sb_block(q, k_ref[...], v_ref[...], u_ref[:t, :t], jnp.zeros((t, 1), F32), _causal_mask(t))
    for j in reversed(range(past // pblk)):
        kb = pk_ref[j * pblk:(j + 1) * pblk, :].astype(BF16)
        vb = pv_ref[j * pblk:(j + 1) * pblk, :].astype(BF16)
        out, carry = _sb_block(q, kb, vb, u_ref[:pblk, :pblk], carry, None)
        acc = acc + out
    o_ref[...] = acc.astype(BF16)


def _sb_sample(sq, skb, svb, past_k, past_v, batch, seq, past):
    pblk = _pick_tile(past, (SB_BLOCK, 128, 64))
    ublk = max(pblk, seq)
    cur = lambda b, h: (b, h)
    return pl.pallas_call(
        functools.partial(_sb_sample_kernel, pblk),
        grid=(batch, SB_HEADS),
        in_specs=[pl.BlockSpec((seq, SB_D), cur),
                  pl.BlockSpec((seq, SB_D), cur),
                  pl.BlockSpec((seq, SB_D), cur),
                  pl.BlockSpec((past, SB_D), cur),
                  pl.BlockSpec((past, SB_D), cur),
                  pl.BlockSpec((ublk, ublk), lambda b, h: (0, 0))],
        out_specs=pl.BlockSpec((seq, SB_D), cur),
        out_shape=jax.ShapeDtypeStruct((batch * seq, SB_W), BF16),
        compiler_params=_params("parallel", "parallel"),
        name="sb_sample",
    )(sq, skb, svb, past_k, past_v, _suffix_matrix(ublk))


def _merge_kernel(x_ref, r_ref, sb_ref, gt_ref, wr_ref, ws_ref, wo_ref, g2_ref, x1_ref, h2t_ref):
    rb = jnp.dot(r_ref[...], wr_ref[...], preferred_element_type=F32)
    sb = jnp.dot(sb_ref[...], ws_ref[...], preferred_element_type=F32)
    mixed = gt_ref[:, :D_MODEL].astype(F32) * rb + gt_ref[:, D_MODEL:].astype(F32) * sb
    x1 = x_ref[...] + jnp.dot(mixed.astype(BF16), wo_ref[...], preferred_element_type=F32)
    x1_ref[...] = x1
    h2 = x1 * lax.rsqrt(jnp.mean(x1 * x1, axis=-1, keepdims=True) + EPS) * g2_ref[...]
    h2t_ref[...] = h2.T.astype(BF16)


def _merge(x2d, r, sbo, gates, w_ret_o, w_sb_o, w_out, norm2_w):
    n = x2d.shape[0]
    tm = _pick_tile(n, (512, 256, 128))
    row = lambda i: (i, 0)
    const = lambda i: (0, 0)
    return pl.pallas_call(
        _merge_kernel,
        grid=(n // tm,),
        in_specs=[pl.BlockSpec((tm, D_MODEL), row),
                  pl.BlockSpec((tm, RET_V_W), row),
                  pl.BlockSpec((tm, SB_W), row),
                  pl.BlockSpec((tm, 2 * D_MODEL), row),
                  pl.BlockSpec((RET_V_W, D_MODEL), const),
                  pl.BlockSpec((SB_W, D_MODEL), const),
                  pl.BlockSpec((D_MODEL, D_MODEL), const),
                  pl.BlockSpec((1, D_MODEL), const)],
        out_specs=[pl.BlockSpec((tm, D_MODEL), row),
                   pl.BlockSpec((D_MODEL, tm), lambda i: (0, i))],
        out_shape=[jax.ShapeDtypeStruct((n, D_MODEL), F32),
                   jax.ShapeDtypeStruct((D_MODEL, n), BF16)],
        compiler_params=_params("parallel"),
        name="merge",
    )(x2d, r, sbo, gates, w_ret_o, w_sb_o, w_out, norm2_w.reshape(1, D_MODEL))


PEER_SLAB = 1024


def _sort_desc(a):
    a = list(a)
    n = len(a)
    k = 2
    while k <= n:
        j = k // 2
        while j >= 1:
            for i in range(n):
                l = i ^ j
                if l > i:
                    hi = jnp.maximum(a[i], a[l])
                    lo = jnp.minimum(a[i], a[l])
                    a[i], a[l] = (hi, lo) if (i & k) == 0 else (lo, hi)
            j //= 2
        k *= 2
    return a


def _merge_bitonic_desc(a):
    a = list(a)
    n = len(a)
    j = n // 2
    while j >= 1:
        for i in range(n):
            l = i ^ j
            if l > i:
                a[i], a[l] = jnp.maximum(a[i], a[l]), jnp.minimum(a[i], a[l])
        j //= 2
    return a


def _top_merge(a, b):
    n = len(a)
    return _merge_bitonic_desc([jnp.maximum(a[i], b[n - 1 - i]) for i in range(n)])


def _top16_rows(s):
    groups = s.shape[0] // 8
    a = _sort_desc([s[8 * j:8 * j + 8, :] for j in range(groups)])
    for shift in (4, 2, 1):
        a = _top_merge(a, [pltpu.roll(x, shift, 0) for x in a])
    return a


def _twin_bf16_words(x):
    hi = pltpu.bitcast(x.astype(BF16).astype(F32), jnp.uint32)
    return hi | (hi >> 16)


def _lookup_desc(keys, vals, default, s):
    rows, tn = s.shape
    s3 = s.reshape(rows // 8, 8, tn)
    out = jnp.full_like(s3, default)
    for b in reversed(range(len(keys))):
        v = vals[b] if isinstance(vals[b], float) else vals[b][None]
        out = jnp.where(s3 == keys[b][None], v, out)
    return out.reshape(rows, tn)


def _peer_kernel(n_slabs, h2t_ref, x1_ref, wq_ref, keys_ref, u_ref, vt_ref, gf_ref, y_ref,
                 s1_ref, s2_ref, cnt_ref, e1_ref, rank2_ref, e2_ref, coef0_ref, coef1_ref, gate0_ref, gate1_ref,
                 acc_ref):
    j = pl.program_id(1)
    ns = n_slabs
    coef_refs = (coef0_ref, coef1_ref)
    gate_refs = (gate0_ref, gate1_ref)
    tn = h2t_ref.shape[1]
    ht = h2t_ref[...]

    @pl.when(j == 0)
    def _():
        qt = jnp.dot(wq_ref[...], ht, preferred_element_type=F32).astype(BF16)
        for p, dst in ((0, s1_ref), (1, s2_ref)):
            for hh in range(PEER_HEADS):
                r0 = (p * PEER_HEADS + hh) * PEER_HALF
                dst[hh] = jnp.dot(keys_ref[p, hh], qt[r0:r0 + PEER_HALF, :], preferred_element_type=F32)

        lw = 128

        def selection_tables(ci, _):
            lanes = pl.ds(pl.multiple_of(ci * lw, lw), lw)
            sub = lax.broadcasted_iota(jnp.int32, (8, lw), 0)
            v1 = [jnp.zeros((8, lw), F32) for _ in range(PEER_TOPK)]
            v2 = [jnp.zeros((8, lw), F32) for _ in range(PEER_TOPK)]
            ranks = [float(b) for b in range(PEER_TOPK)]
            for hh in range(PEER_HEADS):
                top = _top16_rows(s1_ref[hh, :, lanes])
                v1 = [jnp.where(sub == hh, top[a], v1[a]) for a in range(PEER_TOPK)]
                s2 = s2_ref[hh, :, lanes]
                top = _top16_rows(s2)
                v2 = [jnp.where(sub == hh, top[a], v2[a]) for a in range(PEER_TOPK)]
                rank2_ref[hh, :, lanes] = _lookup_desc(top, ranks, float(PEER_TOPK), s2).astype(BF16)
                e2 = jnp.exp(s2.reshape(N_KEYS // 8, 8, lw) - top[0][None])
                e2_ref[hh, :, lanes] = e2.reshape(N_KEYS, lw).astype(BF16)
            cand_rows = [[v1[a] + v2[b] for b in range(PEER_TOPK // (a + 1))] for a in range(PEER_TOPK)]
            cands = [c for row in cand_rows for c in row]
            pad = [jnp.full((8, lw), NEG_BIG, F32)] * (-len(cands) % PEER_TOPK)
            padded = cands + pad
            lists = [_sort_desc(padded[i:i + PEER_TOPK]) for i in range(0, len(padded), PEER_TOPK)]
            while len(lists) > 1:
                lists = [_top_merge(lists[i], lists[i + 1]) for i in range(0, len(lists), 2)]
            tau = lists[0][PEER_TOPK - 1]
            m = cands[0]
            z = jnp.zeros((8, lw), F32)
            for cnd in cands:
                z = z + jnp.where(cnd >= tau, jnp.exp(cnd - m), 0.0)
            shift = v1[0] + jnp.log(z)
            cnt_rows = []
            for row in cand_rows:
                n_sel = jnp.zeros((8, lw), F32)
                for cnd in row:
                    n_sel = n_sel + jnp.where(cnd >= tau, 1.0, 0.0)
                cnt_rows.append(n_sel)
            for hh in range(PEER_HEADS):
                s1 = s1_ref[hh, :, lanes]
                head_row = lambda x: jnp.broadcast_to(x[hh:hh + 1, :], (8, lw))
                cnt = _lookup_desc([head_row(v) for v in v1], [head_row(c) for c in cnt_rows], 0.0, s1)
                cnt_ref[hh, :, lanes] = _twin_bf16_words(cnt)
                e1_ref[hh, :, lanes] = _twin_bf16_words(jnp.exp(s1 - shift[hh:hh + 1, :]))
            return 0

        lax.fori_loop(0, tn // lw, selection_tables, 0)
        acc_ref[...] = jnp.zeros_like(acc_ref)
        coef_refs[1][...] = jnp.zeros_like(coef_refs[1])

    n_i1 = u_ref.shape[0] // N_KEYS
    pack = 16

    def gate_rows(slab):
        out = []
        for ii in range(n_i1):
            i1 = jnp.minimum(slab * n_i1 + ii, N_KEYS - 1)
            out.append([(cnt_ref[hh, pl.ds(i1, 1), :], e1_ref[hh, pl.ds(i1, 1), :])
                        for hh in range(PEER_HEADS)])
        return out

    def gate_group(dst_ref, table_rows, ii):
        as_tile = lambda row: pltpu.bitcast(jnp.broadcast_to(row, (pack // 2, tn)), BF16)
        cnt_b = [as_tile(c) for c, _ in table_rows[ii]]
        e1_b = [as_tile(e) for _, e in table_rows[ii]]
        for c in range(N_KEYS // pack):
            rows = slice(c * pack, (c + 1) * pack)
            g = jnp.zeros((pack, tn), BF16)
            for hh in range(PEER_HEADS):
                w = e2_ref[hh, rows, :] * e1_b[hh]
                g = g + jnp.where(rank2_ref[hh, rows, :] < cnt_b[hh], w, jnp.zeros_like(w))
            r0 = ii * N_KEYS + c * pack
            dst_ref[r0:r0 + pack, :] = g

    @pl.when(j == 0)
    def _():
        rows0 = gate_rows(0)
        for ii in range(n_i1):
            gate_group(gate_refs[0], rows0, ii)

    def add_previous_slab(prev_ref):
        acc_ref[...] += jnp.dot(vt_ref[...], prev_ref[...], preferred_element_type=F32)

    def build_slab(parity):
        cur_ref, prev_ref = coef_refs[parity], coef_refs[1 - parity]
        gate_ref, next_gate_ref = gate_refs[parity], gate_refs[1 - parity]
        next_rows = gate_rows(j + 1)
        n_split = 2 if tn % 512 == 0 else 1
        wn = tn // n_split
        m_split = 2
        em = u_ref.shape[0] // m_split
        dm = D_MODEL // m_split
        gm = em // N_KEYS
        assert n_i1 == m_split * gm

        dots = {}

        def expert_dot_quadrant(mi, ni):
            dots[mi, ni] = jnp.dot(u_ref[mi * em:(mi + 1) * em, :], h2t_ref[:, ni * wn:(ni + 1) * wn],
                                   preferred_element_type=F32)

        def previous_slab_quadrant(mi, ni):
            blk = (slice(mi * dm, (mi + 1) * dm), slice(ni * wn, (ni + 1) * wn))
            acc_ref[blk] += jnp.dot(vt_ref[mi * dm:(mi + 1) * dm, :], prev_ref[:, ni * wn:(ni + 1) * wn],
                                    preferred_element_type=F32)

        def coef_group(ii):
            mi, local = divmod(ii, gm)
            for c in range(N_KEYS // pack):
                for ni in range(n_split):
                    lr = local * N_KEYS + c * pack
                    act = jax.nn.gelu(dots[mi, ni][lr:lr + pack, :].astype(BF16))
                    r0 = ii * N_KEYS + c * pack
                    cols = slice(ni * wn, (ni + 1) * wn)
                    cur_ref[r0:r0 + pack, cols] = gate_ref[r0:r0 + pack, cols] * act

        quads = [(mi, ni) for mi in range(m_split) for ni in range(n_split)]
        matmuls = ([functools.partial(expert_dot_quadrant, *q) for q in quads]
                   + [functools.partial(previous_slab_quadrant, *q) for q in quads])
        tasks = [[] for _ in matmuls]
        first_coef_slot = n_split
        for ii in range(n_i1):
            slot = min(first_coef_slot + ii // 2, len(tasks) - 1)
            slot = max(slot, (ii // gm + 1) * n_split)
            tasks[min(slot, len(tasks) - 1)].append(functools.partial(coef_group, ii))
        for ii in range(n_i1):
            tasks[ii * len(tasks) // n_i1].append(functools.partial(gate_group, next_gate_ref, next_rows, ii))
        for mm, work in zip(matmuls, tasks):
            mm()
            for w in work:
                w()

    for parity in range(2):
        @pl.when(jnp.logical_and(j < ns, j % 2 == parity))
        def _():
            build_slab(parity)

    @pl.when(j == ns)
    def _():
        add_previous_slab(coef_refs[(n_slabs - 1) % 2])
        x2 = x1_ref[...] + acc_ref[...].T
        y = x2 * lax.rsqrt(jnp.mean(x2 * x2, axis=-1, keepdims=True) + EPS) * gf_ref[...]
        y_ref[...] = y


def _peer(h2t, x1, wq_t, keys_bf, u_bf, vt_bf, norm_f_w):
    n = x1.shape[0]
    tn = _pick_tile(n, (512, 256, 128))
    ns = N_EXPERTS // PEER_SLAB
    table = (PEER_HEADS, N_KEYS, tn)
    return pl.pallas_call(
        functools.partial(_peer_kernel, ns),
        grid=(n // tn, ns + 1),
        in_specs=[pl.BlockSpec((D_MODEL, tn), lambda i, j: (0, i)),
                  pl.BlockSpec((tn, D_MODEL), lambda i, j: (i, 0)),
                  pl.BlockSpec((2 * PEER_HEADS * PEER_HALF, D_MODEL), lambda i, j: (0, 0)),
                  pl.BlockSpec((2, PEER_HEADS, N_KEYS, PEER_HALF), lambda i, j: (0, 0, 0, 0)),
                  pl.BlockSpec((PEER_SLAB, D_MODEL), lambda i, j: (jnp.minimum(j, ns - 1), 0)),
                  pl.BlockSpec((D_MODEL, PEER_SLAB), lambda i, j: (0, jnp.maximum(j - 1, 0))),
                  pl.BlockSpec((1, D_MODEL), lambda i, j: (0, 0))],
        out_specs=pl.BlockSpec((tn, D_MODEL), lambda i, j: (i, 0)),
        out_shape=jax.ShapeDtypeStruct((n, D_MODEL), F32),
        scratch_shapes=[pltpu.VMEM(table, F32),
                        pltpu.VMEM(table, F32),
                        pltpu.VMEM(table, jnp.uint32),
                        pltpu.VMEM(table, jnp.uint32),
                        pltpu.VMEM(table, BF16),
                        pltpu.VMEM(table, BF16),
                        pltpu.VMEM((PEER_SLAB, tn), BF16),
                        pltpu.VMEM((PEER_SLAB, tn), BF16),
                        pltpu.VMEM((PEER_SLAB, tn), BF16),
                        pltpu.VMEM((PEER_SLAB, tn), BF16),
                        pltpu.VMEM((D_MODEL, tn), F32)],
        compiler_params=_params("parallel", "arbitrary"),
        name="peer",
    )(h2t, x1, wq_t, keys_bf, u_bf, vt_bf, norm_f_w.reshape(1, D_MODEL))


def _layer(x, pos, past_k, past_v, s0, wts):
    batch, seq, _ = x.shape
    n = batch * seq
    x2d = x.reshape(n, D_MODEL)
    cos_tab, sin_tab = _rotary_tables(pos)
    tm = _pick_tile(n, (512, 256, 128, 64))
    if seq < tm:
        cos_tab = jnp.tile(cos_tab, (tm // seq, 1))
        sin_tab = jnp.tile(sin_tab, (tm // seq, 1))
    rq, rk, rv, gate, sq, sk, sv, skb, svb, gates = _in_proj(x2d, wts["norm1_w"], wts["w_in"], cos_tab, sin_tab)
    r, s_fin = _retention(rq, rk, rv, gate, s0, wts["ret_gn_w"], batch, seq)
    if past_k is None:
        sbo = _sb_prompt(sq, skb, svb, batch, seq)
    else:
        sbo = _sb_sample(sq, skb, svb, past_k, past_v, batch, seq, past_k.shape[0] // batch)
    x1, h2t = _merge(x2d, r, sbo, gates, wts["w_ret_o"], wts["w_sb_o"], wts["w_out"], wts["norm2_w"])
    y = _peer(h2t, x1, wts["w_pq_t"], wts["pk_keys"], wts["peer_u"], wts["peer_v_t"], wts["norm_f_w"])
    return (y.reshape(batch, seq, D_MODEL), sk.reshape(batch, seq, SB_HEADS, SB_D),
            sv.reshape(batch, seq, SB_HEADS, SB_D), s_fin)


def kernel(x_prompt, x_sample, cache_sb_k, cache_sb_v, state_ret, norm1_w, w_in, ret_gn_w, w_ret_o, w_sb_o,
           w_out, norm2_w, w_pq, pk_keys, peer_u, peer_v, norm_f_w):
    assert norm1_w.shape[0] == 1, "single-layer model"
    wq_t = w_pq[0].reshape(D_MODEL, PEER_HEADS, 2, PEER_HALF).transpose(2, 1, 3, 0)
    wq_t = wq_t.reshape(2 * PEER_HEADS * PEER_HALF, D_MODEL).astype(BF16)
    wts = dict(
        norm1_w=norm1_w[0], w_in=w_in[0].astype(BF16), ret_gn_w=ret_gn_w[0],
        w_ret_o=w_ret_o[0].astype(BF16), w_sb_o=w_sb_o[0].astype(BF16), w_out=w_out[0].astype(BF16),
        norm2_w=norm2_w[0], w_pq_t=wq_t, pk_keys=pk_keys[0].astype(BF16),
        peer_u=peer_u[0].astype(BF16), peer_v_t=peer_v[0].T.astype(BF16), norm_f_w=norm_f_w)
    bp, tp, _ = x_prompt.shape
    bs, ts, _ = x_sample.shape
    past = cache_sb_k.shape[2]
    yp, kp, vp, sp = _layer(x_prompt, jnp.arange(tp, dtype=jnp.int32), None, None,
                            jnp.zeros((bp, RET_HEADS, RET_DK, RET_DV), F32), wts)
    ys, ks, vs, ss = _layer(x_sample, past + jnp.arange(ts, dtype=jnp.int32),
                            cache_sb_k[0].reshape(bs * past, SB_W), cache_sb_v[0].reshape(bs * past, SB_W),
                            state_ret[0], wts)
    return (yp, ys, kp[None], vp[None], sp[None], ks[None], vs[None], ss[None])
```

```python
import functools
import math

import numpy as np
import jax
import jax.numpy as jnp
from jax import lax
from jax.experimental import pallas as pl
from jax.experimental.pallas import tpu as pltpu

F32 = jnp.float32
BF16 = jnp.bfloat16

D_MODEL = 1024
CHUNK = 64
RET_HEADS = 4
RET_DK = 128
RET_DV = 256
SB_HEADS = 4
SB_D = 128
N_KEYS = 128
N_EXPERTS = N_KEYS * N_KEYS
PEER_HEADS = 8
PEER_HALF = 128
PEER_TOPK = 16
ROPE_BASE = 10000.0
EPS = 1e-6
RET_QK_W = RET_HEADS * RET_DK
RET_V_W = RET_HEADS * RET_DV
SB_W = SB_HEADS * SB_D
IN_W = 2 * RET_QK_W + 2 * RET_V_W + 3 * SB_W + 2 * D_MODEL

VMEM_LIMIT_BYTES = 56 * 1024 * 1024
NEG_BIG = -1e30


def _pick_tile(n, candidates):
    for c in candidates:
        if n % c == 0:
            return c
    raise ValueError(f"no tile in {candidates} divides {n}")


def _params(*sem):
    return pltpu.CompilerParams(dimension_semantics=sem, vmem_limit_bytes=VMEM_LIMIT_BYTES)


def _in_proj_kernel(x_ref, g_ref, w_ref, cos_ref, sin_ref,
                    rq_ref, rk_ref, rv_ref, rg_ref, sq_ref, sk_ref, sv_ref, skb_ref, svb_ref, gt_ref):
    x = x_ref[...]
    h = (x * lax.rsqrt(jnp.mean(x * x, axis=-1, keepdims=True) + EPS) * g_ref[...]).astype(BF16)

    def proj(lo, width):
        return jnp.dot(h, w_ref[:, lo:lo + width], preferred_element_type=F32)

    c = cos_ref[...]
    s = sin_ref[...]

    def rotary_store(p, out_ref, scale):
        for hh in range(RET_HEADS):
            ph = p[:, hh * RET_DK:(hh + 1) * RET_DK]
            r = ph * c + pltpu.roll(ph, RET_DK // 2, 1) * s
            if scale != 1.0:
                r = r * scale
            out_ref[:, hh * RET_DK:(hh + 1) * RET_DK] = r.astype(BF16)

    rotary_store(proj(0, RET_QK_W), rq_ref, 1.0)
    rotary_store(proj(RET_QK_W, RET_QK_W), rk_ref, RET_DK ** -0.5)
    off = 2 * RET_QK_W
    for j in range(RET_V_W // 512):
        rv_ref[:, j * 512:(j + 1) * 512] = proj(off + j * 512, 512).astype(BF16)
    off += RET_V_W
    for j in range(RET_V_W // 512):
        g = proj(off + j * 512, 512)
        rg_ref[:, j * 512:(j + 1) * 512] = (g / (1.0 + jnp.exp(-g))).astype(BF16)
    off += RET_V_W
    sq_ref[...] = (proj(off, SB_W) * (SB_D ** -0.5 * LOG2E)).astype(BF16)
    off += SB_W
    tm = x_ref.shape[0]

    def store_rows_by_head(out_ref, val):
        for hh in range(SB_HEADS):
            out_ref[pl.ds(hh, tm, stride=SB_HEADS), :] = val[:, hh * SB_D:(hh + 1) * SB_D]

    k = proj(off, SB_W)
    store_rows_by_head(sk_ref, k)
    skb_ref[...] = k.astype(BF16)
    off += SB_W
    v = proj(off, SB_W)
    store_rows_by_head(sv_ref, v)
    svb_ref[...] = v.astype(BF16)
    off += SB_W
    for j in range(2 * D_MODEL // 512):
        g = proj(off + j * 512, 512)
        gt_ref[:, j * 512:(j + 1) * 512] = (1.0 / (1.0 + jnp.exp(-g))).astype(BF16)


def _in_proj(x2d, norm_w, w_in_bf, cos_tab, sin_tab):
    n = x2d.shape[0]
    tm = _pick_tile(n, (512, 256, 128, 64))
    assert cos_tab.shape[0] % tm == 0
    npos = cos_tab.shape[0] // tm
    row = lambda i: (i, 0)
    const = lambda i: (0, 0)
    pos = lambda i: (i % npos, 0)
    outs = [(1, RET_QK_W, BF16), (1, RET_QK_W, BF16), (1, RET_V_W, BF16), (1, RET_V_W, BF16), (1, SB_W, BF16),
            (SB_HEADS, SB_D, F32), (SB_HEADS, SB_D, F32), (1, SB_W, BF16), (1, SB_W, BF16),
            (1, 2 * D_MODEL, BF16)]
    return pl.pallas_call(
        _in_proj_kernel,
        grid=(n // tm,),
        in_specs=[pl.BlockSpec((tm, D_MODEL), row),
                  pl.BlockSpec((1, D_MODEL), const),
                  pl.BlockSpec((D_MODEL, IN_W), const),
                  pl.BlockSpec((tm, RET_DK), pos),
                  pl.BlockSpec((tm, RET_DK), pos)],
        out_specs=[pl.BlockSpec((tm * r, w), row) for r, w, _ in outs],
        out_shape=[jax.ShapeDtypeStruct((n * r, w), dt) for r, w, dt in outs],
        compiler_params=_params("parallel"),
        name="in_proj",
    )(x2d, norm_w.reshape(1, D_MODEL), w_in_bf, cos_tab, sin_tab)


def _rotary_tables(pos):
    inv = ROPE_BASE ** (-jnp.arange(0, RET_DK, 2, dtype=F32) / RET_DK)
    ang = pos.astype(F32)[:, None] * inv[None, :]
    cos = jnp.cos(ang)
    sin = jnp.sin(ang)
    return jnp.concatenate([cos, cos], axis=1), jnp.concatenate([-sin, sin], axis=1)


def _retention_tables(blk):
    log_g = np.log(1.0 - 2.0 ** (-5.0 - np.arange(RET_HEADS, dtype=np.float64)))
    i = np.arange(blk, dtype=np.float64)
    dist = np.abs(i[:, None] - i[None, :])
    visible = (np.arange(blk)[None, :] // CHUNK) <= (np.arange(blk)[:, None] // CHUNK)
    dmat = np.where(visible[None], np.exp(dist[None] * log_g[:, None, None]), 0.0)
    inter = np.exp((i[None, :] + 1.0) * log_g[:, None])
    kdec = np.exp((blk - 1.0 - i)[None, :] * log_g[:, None])
    inter = np.broadcast_to(inter[:, :, None], (RET_HEADS, blk, RET_DV))
    kdec = np.broadcast_to(kdec[:, :, None], (RET_HEADS, blk, RET_DK))
    sdec = tuple(float(np.exp(blk * g)) for g in log_g)
    return (jnp.asarray(dmat, F32), jnp.asarray(inter, F32), jnp.asarray(kdec, F32)), sdec


def _retention_kernel(sdec, q_ref, k_ref, v_ref, gate_ref, s0_ref, dmat_ref, inter_ref, kdec_ref, gn_ref,
                      r_ref, sfin_ref, state_ref):
    t = pl.program_id(1)

    @pl.when(t == 0)
    def _():
        state_ref[...] = s0_ref[0]

    for hh in range(RET_HEADS):
        q = q_ref[:, hh * RET_DK:(hh + 1) * RET_DK]
        k = k_ref[:, hh * RET_DK:(hh + 1) * RET_DK]
        v = v_ref[:, hh * RET_DV:(hh + 1) * RET_DV]
        state = state_ref[hh]
        sc = lax.dot_general(q, k, (((1,), (1,)), ((), ())), preferred_element_type=F32)
        p = (sc * dmat_ref[hh]).astype(BF16)
        o = jnp.dot(p, v, preferred_element_type=F32)
        o = o + jnp.dot(q, state.astype(BF16), preferred_element_type=F32) * inter_ref[hh]
        kd = (k.astype(F32) * kdec_ref[hh]).astype(BF16)
        upd = lax.dot_general(kd, v, (((0,), (0,)), ((), ())), preferred_element_type=F32)
        state_ref[hh] = sdec[hh] * state + upd
        mu = jnp.mean(o, axis=-1, keepdims=True)
        cen = o - mu
        var = jnp.mean(cen * cen, axis=-1, keepdims=True)
        r = cen * lax.rsqrt(var + EPS) * gn_ref[:, hh * RET_DV:(hh + 1) * RET_DV]
        r = r * gate_ref[:, hh * RET_DV:(hh + 1) * RET_DV].astype(F32)
        r_ref[:, hh * RET_DV:(hh + 1) * RET_DV] = r.astype(BF16)

    @pl.when(t == pl.num_programs(1) - 1)
    def _():
        sfin_ref[0] = state_ref[...]


def _retention(rq, rk, rv, gate, s0, gn_w, batch, seq):
    blk = _pick_tile(seq, (256, 128, 64))
    nt = seq // blk
    tabs, sdec = _retention_tables(blk)
    row = lambda b, t: (b * nt + t, 0)
    per_b = lambda b, t: (b, 0, 0, 0)
    const3 = lambda b, t: (0, 0, 0)
    return pl.pallas_call(
        functools.partial(_retention_kernel, sdec),
        grid=(batch, nt),
        in_specs=[pl.BlockSpec((blk, RET_QK_W), row),
                  pl.BlockSpec((blk, RET_QK_W), row),
                  pl.BlockSpec((blk, RET_V_W), row),
                  pl.BlockSpec((blk, RET_V_W), row),
                  pl.BlockSpec((1, RET_HEADS, RET_DK, RET_DV), per_b),
                  pl.BlockSpec((RET_HEADS, blk, blk), const3),
                  pl.BlockSpec((RET_HEADS, blk, RET_DV), const3),
                  pl.BlockSpec((RET_HEADS, blk, RET_DK), const3),
                  pl.BlockSpec((1, RET_V_W), lambda b, t: (0, 0))],
        out_specs=[pl.BlockSpec((blk, RET_V_W), row),
                   pl.BlockSpec((1, RET_HEADS, RET_DK, RET_DV), per_b)],
        out_shape=[jax.ShapeDtypeStruct((batch * seq, RET_V_W), BF16),
                   jax.ShapeDtypeStruct((batch, RET_HEADS, RET_DK, RET_DV), F32)],
        scratch_shapes=[pltpu.VMEM((RET_HEADS, RET_DK, RET_DV), F32)],
        compiler_params=_params("parallel", "arbitrary"),
        name="retention",
    )(rq, rk, rv, gate, s0, *tabs, gn_w.reshape(1, RET_V_W))


SB_BLOCK = 256
SB_UNROLL = 4
LOG2E = math.log2(math.e)
SOFTPLUS_CLAMP = 30.0 * LOG2E


def _softplus2(z):
    return jnp.maximum(jnp.log(1.0 + jnp.exp2(jnp.minimum(z, SOFTPLUS_CLAMP))) * LOG2E, z)


def _suffix_matrix(n):
    return jnp.asarray(np.tril(np.ones((n, n), np.float32), -1), BF16)


def _sb_block(q, kblk, vblk, umat, carry, mask):
    z = lax.dot_general(q, kblk, (((1,), (1,)), ((), ())), preferred_element_type=F32)
    sp = _softplus2(z)
    spm = sp if mask is None else jnp.where(mask, sp, 0.0)
    after = jnp.dot(spm.astype(BF16), umat, preferred_element_type=F32)
    w = jnp.exp2(z - sp - after - carry)
    if mask is not None:
        w = jnp.where(mask, w, 0.0)
    out = jnp.dot(w.astype(BF16), vblk, preferred_element_type=F32)
    return out, carry + after[:, 0:1] + spm[:, 0:1]


def _causal_mask(n):
    r = lax.broadcasted_iota(jnp.int32, (n, n), 0)
    c = lax.broadcasted_iota(jnp.int32, (n, n), 1)
    return c < r


def _sb_prompt_kernel(q_ref, k_ref, v_ref, u_ref, o_ref, acc_ref, carry_ref):
    qi = pl.program_id(2)
    blk = q_ref.shape[0]
    q = q_ref[...]
    umat = u_ref[...]

    def sweep(first, n, diagonal):
        s = pl.multiple_of((first - n + 1) * blk, blk)
        keys = k_ref[pl.ds(s, n * blk), :]
        vals = v_ref[pl.ds(s, n * blk), :]
        z = lax.dot_general(q, keys, (((1,), (1,)), ((), ())), preferred_element_type=F32)
        sp = _softplus2(z)
        fail = [sp[:, u * blk:(u + 1) * blk] for u in range(n)]
        if diagonal:
            mask = _causal_mask(blk)
            fail[-1] = jnp.where(mask, fail[-1], 0.0)
        after = jnp.dot(jnp.concatenate(fail, axis=0).astype(BF16), umat, preferred_element_type=F32)
        carry = jnp.zeros((blk, 1), F32) if diagonal else carry_ref[...]
        offsets = [None] * n
        for u in reversed(range(n)):
            offsets[u] = carry
            carry = carry + after[u * blk:(u + 1) * blk, 0:1] + fail[u][:, 0:1]
        later = jnp.concatenate([after[u * blk:(u + 1) * blk, :] + offsets[u] for u in range(n)], axis=1)
        w = jnp.exp2(z - sp - later)
        if diagonal:
            near = jnp.where(mask, w[:, (n - 1) * blk:], 0.0)
            w = near if n == 1 else jnp.concatenate([w[:, :(n - 1) * blk], near], axis=1)
        out = jnp.dot(w.astype(BF16), vals, preferred_element_type=F32)
        if diagonal:
            acc_ref[...] = out
        else:
            acc_ref[...] += out
        carry_ref[...] = carry

    lead = qi % SB_UNROLL + 1
    for n in range(1, SB_UNROLL + 1):
        @pl.when(lead == n)
        def _():
            sweep(qi, n, True)

    def main_body(i, _):
        sweep(qi - lead - i * SB_UNROLL, SB_UNROLL, False)
        return 0

    lax.fori_loop(0, qi // SB_UNROLL, main_body, 0)
    o_ref[...] = acc_ref[...].astype(BF16)


def _sb_prompt(sq, skb, svb, batch, seq):
    blk = _pick_tile(seq, (SB_BLOCK, 128, 64))
    nq = seq // blk
    qmap = lambda b, h, i: (b * nq + i, h)
    kvmap = lambda b, h, i: (b, h)
    return pl.pallas_call(
        _sb_prompt_kernel,
        grid=(batch, SB_HEADS, nq),
        in_specs=[pl.BlockSpec((blk, SB_D), qmap),
                  pl.BlockSpec((seq, SB_D), kvmap),
                  pl.BlockSpec((seq, SB_D), kvmap),
                  pl.BlockSpec((blk, blk), lambda b, h, i: (0, 0))],
        out_specs=pl.BlockSpec((blk, SB_D), qmap),
        out_shape=jax.ShapeDtypeStruct((batch * seq, SB_W), BF16),
        scratch_shapes=[pltpu.VMEM((blk, SB_D), F32), pltpu.VMEM((blk, 1), F32)],
        compiler_params=_params("parallel", "parallel", "arbitrary"),
        name="sb_prompt",
    )(sq, skb, svb, _suffix_matrix(blk))


def _sb_sample_kernel(pblk, q_ref, k_ref, v_ref, pk_ref, pv_ref, u_ref, o_ref):
    t = q_ref.shape[0]
    past = pk_ref.shape[0]
    q = q_ref[...]
    acc, carry = _sb_block(q, k_ref[...], v_ref[...], u_ref[:t, :t], jnp.zeros((t, 1), F32), _causal_mask(t))
    for j in reversed(range(past // pblk)):
        kb = pk_ref[j * pblk:(j + 1) * pblk, :].astype(BF16)
        vb = pv_ref[j * pblk:(j + 1) * pblk, :].astype(BF16)
        out, carry = _sb_block(q, kb, vb, u_ref[:pblk, :pblk], carry, None)
        acc = acc + out
    o_ref[...] = acc.astype(BF16)


def _sb_sample(sq, skb, svb, past_k, past_v, batch, seq, past):
    pblk = _pick_tile(past, (SB_BLOCK, 128, 64))
    ublk = max(pblk, seq)
    cur = lambda b, h: (b, h)
    return pl.pallas_call(
        functools.partial(_sb_sample_kernel, pblk),
        grid=(batch, SB_HEADS),
        in_specs=[pl.BlockSpec((seq, SB_D), cur),
                  pl.BlockSpec((seq, SB_D), cur),
                  pl.BlockSpec((seq, SB_D), cur),
                  pl.BlockSpec((past, SB_D), cur),
                  pl.BlockSpec((past, SB_D), cur),
                  pl.BlockSpec((ublk, ublk), lambda b, h: (0, 0))],
        out_specs=pl.BlockSpec((seq, SB_D), cur),
        out_shape=jax.ShapeDtypeStruct((batch * seq, SB_W), BF16),
        compiler_params=_params("parallel", "parallel"),
        name="sb_sample",
    )(sq, skb, svb, past_k, past_v, _suffix_matrix(ublk))


def _merge_kernel(x_ref, r_ref, sb_ref, gt_ref, wr_ref, ws_ref, wo_ref, g2_ref, x1_ref, h2t_ref):
    rb = jnp.dot(r_ref[...], wr_ref[...], preferred_element_type=F32)
    sb = jnp.dot(sb_ref[...], ws_ref[...], preferred_element_type=F32)
    mixed = gt_ref[:, :D_MODEL].astype(F32) * rb + gt_ref[:, D_MODEL:].astype(F32) * sb
    x1 = x_ref[...] + jnp.dot(mixed.astype(BF16), wo_ref[...], preferred_element_type=F32)
    x1_ref[...] = x1
    h2 = x1 * lax.rsqrt(jnp.mean(x1 * x1, axis=-1, keepdims=True) + EPS) * g2_ref[...]
    h2t_ref[...] = h2.T.astype(BF16)


def _merge(x2d, r, sbo, gates, w_ret_o, w_sb_o, w_out, norm2_w):
    n = x2d.shape[0]
    tm = _pick_tile(n, (512, 256, 128))
    row = lambda i: (i, 0)
    const = lambda i: (0, 0)
    return pl.pallas_call(
        _merge_kernel,
        grid=(n // tm,),
        in_specs=[pl.BlockSpec((tm, D_MODEL), row),
                  pl.BlockSpec((tm, RET_V_W), row),
                  pl.BlockSpec((tm, SB_W), row),
                  pl.BlockSpec((tm, 2 * D_MODEL), row),
                  pl.BlockSpec((RET_V_W, D_MODEL), const),
                  pl.BlockSpec((SB_W, D_MODEL), const),
                  pl.BlockSpec((D_MODEL, D_MODEL), const),
                  pl.BlockSpec((1, D_MODEL), const)],
        out_specs=[pl.BlockSpec((tm, D_MODEL), row),
                   pl.BlockSpec((D_MODEL, tm), lambda i: (0, i))],
        out_shape=[jax.ShapeDtypeStruct((n, D_MODEL), F32),
                   jax.ShapeDtypeStruct((D_MODEL, n), BF16)],
        compiler_params=_params("parallel"),
        name="merge",
    )(x2d, r, sbo, gates, w_ret_o, w_sb_o, w_out, norm2_w.reshape(1, D_MODEL))


PEER_SLAB = 1024


def _sort_desc(a):
    a = list(a)
    n = len(a)
    k = 2
    while k <= n:
        j = k // 2
        while j >= 1:
            for i in range(n):
                l = i ^ j
                if l > i:
                    hi = jnp.maximum(a[i], a[l])
                    lo = jnp.minimum(a[i], a[l])
                    a[i], a[l] = (hi, lo) if (i & k) == 0 else (lo, hi)
            j //= 2
        k *= 2
    return a


def _merge_bitonic_desc(a):
    a = list(a)
    n = len(a)
    j = n // 2
    while j >= 1:
        for i in range(n):
            l = i ^ j
            if l > i:
                a[i], a[l] = jnp.maximum(a[i], a[l]), jnp.minimum(a[i], a[l])
        j //= 2
    return a


def _top_merge(a, b):
    n = len(a)
    return _merge_bitonic_desc([jnp.maximum(a[i], b[n - 1 - i]) for i in range(n)])


def _top16_rows(s):
    groups = s.shape[0] // 8
    a = _sort_desc([s[8 * j:8 * j + 8, :] for j in range(groups)])
    for shift in (4, 2, 1):
        a = _top_merge(a, [pltpu.roll(x, shift, 0) for x in a])
    return a


def _gelu_tanh(x):
    c0 = math.sqrt(2.0 / math.pi)
    half = 0.5 * x
    return half + half * jnp.tanh(x * (c0 + (c0 * 0.044715) * (x * x)))


def _twin_bf16_words(x):
    hi = pltpu.bitcast(x.astype(BF16).astype(F32), jnp.uint32)
    return hi | (hi >> 16)


def _lookup_desc(keys, vals, default, s):
    rows, tn = s.shape
    s3 = s.reshape(rows // 8, 8, tn)
    out = jnp.full_like(s3, default)
    for b in reversed(range(len(keys))):
        v = vals[b] if isinstance(vals[b], float) else vals[b][None]
        out = jnp.where(s3 == keys[b][None], v, out)
    return out.reshape(rows, tn)


def _peer_kernel(n_slabs, h2t_ref, x1_ref, wq_ref, keys_ref, u_ref, vt_ref, gf_ref, y_ref,
                 s1_ref, s2_ref, cnt_ref, e1_ref, rank2_ref, e2_ref, coef0_ref, coef1_ref, gate0_ref, gate1_ref,
                 acc_ref):
    j = pl.program_id(1)
    ns = n_slabs
    coef_refs = (coef0_ref, coef1_ref)
    gate_refs = (gate0_ref, gate1_ref)
    tn = h2t_ref.shape[1]
    ht = h2t_ref[...]

    @pl.when(j == 0)
    def _():
        qt = jnp.dot(wq_ref[...], ht, preferred_element_type=F32).astype(BF16)
        for p, dst in ((0, s1_ref), (1, s2_ref)):
            for hh in range(PEER_HEADS):
                r0 = (p * PEER_HEADS + hh) * PEER_HALF
                dst[hh] = jnp.dot(keys_ref[p, hh], qt[r0:r0 + PEER_HALF, :], preferred_element_type=F32)

        lw = 128

        def selection_tables(ci, _):
            lanes = pl.ds(pl.multiple_of(ci * lw, lw), lw)
            sub = lax.broadcasted_iota(jnp.int32, (8, lw), 0)
            v1 = [jnp.zeros((8, lw), F32) for _ in range(PEER_TOPK)]
            v2 = [jnp.zeros((8, lw), F32) for _ in range(PEER_TOPK)]
            ranks = [float(b) for b in range(PEER_TOPK)]
            for hh in range(PEER_HEADS):
                top = _top16_rows(s1_ref[hh, :, lanes])
                v1 = [jnp.where(sub == hh, top[a], v1[a]) for a in range(PEER_TOPK)]
                s2 = s2_ref[hh, :, lanes]
                top = _top16_rows(s2)
                v2 = [jnp.where(sub == hh, top[a], v2[a]) for a in range(PEER_TOPK)]
                rank2_ref[hh, :, lanes] = _lookup_desc(top, ranks, float(PEER_TOPK), s2).astype(BF16)
                e2 = jnp.exp(s2.reshape(N_KEYS // 8, 8, lw) - top[0][None])
                e2_ref[hh, :, lanes] = e2.reshape(N_KEYS, lw).astype(BF16)
            cand_rows = [[v1[a] + v2[b] for b in range(PEER_TOPK // (a + 1))] for a in range(PEER_TOPK)]
            cands = [c for row in cand_rows for c in row]
            pad = [jnp.full((8, lw), NEG_BIG, F32)] * (-len(cands) % PEER_TOPK)
            padded = cands + pad
            lists = [_sort_desc(padded[i:i + PEER_TOPK]) for i in range(0, len(padded), PEER_TOPK)]
            while len(lists) > 1:
                lists = [_top_merge(lists[i], lists[i + 1]) for i in range(0, len(lists), 2)]
            tau = lists[0][PEER_TOPK - 1]
            m = cands[0]
            z = jnp.zeros((8, lw), F32)
            for cnd in cands:
                z = z + jnp.where(cnd >= tau, jnp.exp(cnd - m), 0.0)
            shift = v1[0] + jnp.log(z)
            cnt_rows = []
            for row in cand_rows:
                n_sel = jnp.zeros((8, lw), F32)
                for cnd in row:
                    n_sel = n_sel + jnp.where(cnd >= tau, 1.0, 0.0)
                cnt_rows.append(n_sel)
            for hh in range(PEER_HEADS):
                s1 = s1_ref[hh, :, lanes]
                head_row = lambda x: jnp.broadcast_to(x[hh:hh + 1, :], (8, lw))
                cnt = _lookup_desc([head_row(v) for v in v1], [head_row(c) for c in cnt_rows], 0.0, s1)
                cnt_ref[hh, :, lanes] = _twin_bf16_words(cnt)
                e1_ref[hh, :, lanes] = _twin_bf16_words(jnp.exp(s1 - shift[hh:hh + 1, :]))
            return 0

        lax.fori_loop(0, tn // lw, selection_tables, 0)
        acc_ref[...] = jnp.zeros_like(acc_ref)
        coef_refs[1][...] = jnp.zeros_like(coef_refs[1])

    n_i1 = u_ref.shape[0] // N_KEYS
    pack = 16

    def gate_rows(slab):
        out = []
        for ii in range(n_i1):
            i1 = jnp.minimum(slab * n_i1 + ii, N_KEYS - 1)
            out.append([(cnt_ref[hh, pl.ds(i1, 1), :], e1_ref[hh, pl.ds(i1, 1), :])
                        for hh in range(PEER_HEADS)])
        return out

    def gate_group(dst_ref, table_rows, ii):
        as_tile = lambda row: pltpu.bitcast(jnp.broadcast_to(row, (pack // 2, tn)), BF16)
        cnt_b = [as_tile(c) for c, _ in table_rows[ii]]
        e1_b = [as_tile(e) for _, e in table_rows[ii]]
        for c in range(N_KEYS // pack):
            rows = slice(c * pack, (c + 1) * pack)
            g = jnp.zeros((pack, tn), BF16)
            for hh in range(PEER_HEADS):
                w = e2_ref[hh, rows, :] * e1_b[hh]
                g = g + jnp.where(rank2_ref[hh, rows, :] < cnt_b[hh], w, jnp.zeros_like(w))
            r0 = ii * N_KEYS + c * pack
            dst_ref[r0:r0 + pack, :] = g

    @pl.when(j == 0)
    def _():
        rows0 = gate_rows(0)
        for ii in range(n_i1):
            gate_group(gate_refs[0], rows0, ii)

    def add_previous_slab(prev_ref):
        acc_ref[...] += jnp.dot(vt_ref[...], prev_ref[...], preferred_element_type=F32)

    def build_slab(parity):
        cur_ref, prev_ref = coef_refs[parity], coef_refs[1 - parity]
        gate_ref, next_gate_ref = gate_refs[parity], gate_refs[1 - parity]
        next_rows = gate_rows(j + 1)
        n_split = 2 if tn % 512 == 0 else 1
        wn = tn // n_split
        m_split = 2
        em = u_ref.shape[0] // m_split
        dm = D_MODEL // m_split
        gm = em // N_KEYS
        assert n_i1 == m_split * gm

        dots = {}

        def expert_dot_quadrant(mi, ni):
            dots[mi, ni] = jnp.dot(u_ref[mi * em:(mi + 1) * em, :], h2t_ref[:, ni * wn:(ni + 1) * wn],
                                   preferred_element_type=F32)

        def previous_slab_quadrant(mi, ni):
            blk = (slice(mi * dm, (mi + 1) * dm), slice(ni * wn, (ni + 1) * wn))
            acc_ref[blk] += jnp.dot(vt_ref[mi * dm:(mi + 1) * dm, :], prev_ref[:, ni * wn:(ni + 1) * wn],
                                    preferred_element_type=F32)

        def coef_group(ii):
            mi, local = divmod(ii, gm)
            for c in range(N_KEYS // pack):
                for ni in range(n_split):
                    lr = local * N_KEYS + c * pack
                    act = _gelu_tanh(dots[mi, ni][lr:lr + pack, :].astype(BF16))
                    r0 = ii * N_KEYS + c * pack
                    cols = slice(ni * wn, (ni + 1) * wn)
                    cur_ref[r0:r0 + pack, cols] = gate_ref[r0:r0 + pack, cols] * act

        quads = [(mi, ni) for mi in range(m_split) for ni in range(n_split)]
        matmuls = ([functools.partial(expert_dot_quadrant, *q) for q in quads]
                   + [functools.partial(previous_slab_quadrant, *q) for q in quads])
        tasks = [[] for _ in matmuls]
        first_coef_slot = n_split
        for ii in range(n_i1):
            slot = min(first_coef_slot + ii // 2, len(tasks) - 1)
            slot = max(slot, (ii // gm + 1) * n_split)
            tasks[min(slot, len(tasks) - 1)].append(functools.partial(coef_group, ii))
        for ii in range(n_i1):
            tasks[ii * len(tasks) // n_i1].append(functools.partial(gate_group, next_gate_ref, next_rows, ii))
        for mm, work in zip(matmuls, tasks):
            mm()
            for w in work:
                w()

    for parity in range(2):
        @pl.when(jnp.logical_and(j < ns, j % 2 == parity))
        def _():
            build_slab(parity)

    @pl.when(j == ns)
    def _():
        add_previous_slab(coef_refs[(n_slabs - 1) % 2])
        x2 = x1_ref[...] + acc_ref[...].T
        y = x2 * lax.rsqrt(jnp.mean(x2 * x2, axis=-1, keepdims=True) + EPS) * gf_ref[...]
        y_ref[...] = y


def _peer(h2t, x1, wq_t, keys_bf, u_bf, vt_bf, norm_f_w):
    n = x1.shape[0]
    tn = _pick_tile(n, (512, 256, 128))
    ns = N_EXPERTS // PEER_SLAB
    table = (PEER_HEADS, N_KEYS, tn)
    return pl.pallas_call(
        functools.partial(_peer_kernel, ns),
        grid=(n // tn, ns + 1),
        in_specs=[pl.BlockSpec((D_MODEL, tn), lambda i, j: (0, i)),
                  pl.BlockSpec((tn, D_MODEL), lambda i, j: (i, 0)),
                  pl.BlockSpec((2 * PEER_HEADS * PEER_HALF, D_MODEL), lambda i, j: (0, 0)),
                  pl.BlockSpec((2, PEER_HEADS, N_KEYS, PEER_HALF), lambda i, j: (0, 0, 0, 0)),
                  pl.BlockSpec((PEER_SLAB, D_MODEL), lambda i, j: (jnp.minimum(j, ns - 1), 0)),
                  pl.BlockSpec((D_MODEL, PEER_SLAB), lambda i, j: (0, jnp.maximum(j - 1, 0))),
                  pl.BlockSpec((1, D_MODEL), lambda i, j: (0, 0))],
        out_specs=pl.BlockSpec((tn, D_MODEL), lambda i, j: (i, 0)),
        out_shape=jax.ShapeDtypeStruct((n, D_MODEL), F32),
        scratch_shapes=[pltpu.VMEM(table, F32),
                        pltpu.VMEM(table, F32),
                        pltpu.VMEM(table, jnp.uint32),
                        pltpu.VMEM(table, jnp.uint32),
                        pltpu.VMEM(table, BF16),
                        pltpu.VMEM(table, BF16),
                        pltpu.VMEM((PEER_SLAB, tn), BF16),
                        pltpu.VMEM((PEER_SLAB, tn), BF16),
                        pltpu.VMEM((PEER_SLAB, tn), BF16),
                        pltpu.VMEM((PEER_SLAB, tn), BF16),
                        pltpu.VMEM((D_MODEL, tn), F32)],
        compiler_params=_params("parallel", "arbitrary"),
        name="peer",
    )(h2t, x1, wq_t, keys_bf, u_bf, vt_bf, norm_f_w.reshape(1, D_MODEL))


def _layer(x, pos, past_k, past_v, s0, wts):
    batch, seq, _ = x.shape
    n = batch * seq
    x2d = x.reshape(n, D_MODEL)
    cos_tab, sin_tab = _rotary_tables(pos)
    tm = _pick_tile(n, (512, 256, 128, 64))
    if seq < tm:
        cos_tab = jnp.tile(cos_tab, (tm // seq, 1))
        sin_tab = jnp.tile(sin_tab, (tm // seq, 1))
    rq, rk, rv, gate, sq, sk, sv, skb, svb, gates = _in_proj(x2d, wts["norm1_w"], wts["w_in"], cos_tab, sin_tab)
    r, s_fin = _retention(rq, rk, rv, gate, s0, wts["ret_gn_w"], batch, seq)
    if past_k is None:
        sbo = _sb_prompt(sq, skb, svb, batch, seq)
    else:
        sbo = _sb_sample(sq, skb, svb, past_k, past_v, batch, seq, past_k.shape[0] // batch)
    x1, h2t = _merge(x2d, r, sbo, gates, wts["w_ret_o"], wts["w_sb_o"], wts["w_out"], wts["norm2_w"])
    y = _peer(h2t, x1, wts["w_pq_t"], wts["pk_keys"], wts["peer_u"], wts["peer_v_t"], wts["norm_f_w"])
    return (y.reshape(batch, seq, D_MODEL), sk.reshape(batch, seq, SB_HEADS, SB_D),
            sv.reshape(batch, seq, SB_HEADS, SB_D), s_fin)


def kernel(x_prompt, x_sample, cache_sb_k, cache_sb_v, state_ret, norm1_w, w_in, ret_gn_w, w_ret_o, w_sb_o,
           w_out, norm2_w, w_pq, pk_keys, peer_u, peer_v, norm_f_w):
    assert norm1_w.shape[0] == 1, "single-layer model"
    wq_t = w_pq[0].reshape(D_MODEL, PEER_HEADS, 2, PEER_HALF).transpose(2, 1, 3, 0)
    wq_t = wq_t.reshape(2 * PEER_HEADS * PEER_HALF, D_MODEL).astype(BF16)
    wts = dict(
        norm1_w=norm1_w[0], w_in=w_in[0].astype(BF16), ret_gn_w=ret_gn_w[0],
        w_ret_o=w_ret_o[0].astype(BF16), w_sb_o=w_sb_o[0].astype(BF16), w_out=w_out[0].astype(BF16),
        norm2_w=norm2_w[0], w_pq_t=wq_t, pk_keys=pk_keys[0].astype(BF16),
        peer_u=peer_u[0].astype(BF16), peer_v_t=peer_v[0].T.astype(BF16), norm_f_w=norm_f_w)
    bp, tp, _ = x_prompt.shape
    bs, ts, _ = x_sample.shape
    past = cache_sb_k.shape[2]
    yp, kp, vp, sp = _layer(x_prompt, jnp.arange(tp, dtype=jnp.int32), None, None,
                            jnp.zeros((bp, RET_HEADS, RET_DK, RET_DV), F32), wts)
    ys, ks, vs, ss = _layer(x_sample, past + jnp.arange(ts, dtype=jnp.int32),
                            cache_sb_k[0].reshape(bs * past, SB_W), cache_sb_v[0].reshape(bs * past, SB_W),
                            state_ret[0], wts)
    return (yp, ys, kp[None], vp[None], sp[None], ks[None], vs[None], ss[None])
```

```python
import functools
import math

import numpy as np
import jax
import jax.numpy as jnp
from jax import lax
from jax.experimental import pallas as pl
from jax.experimental.pallas import tpu as pltpu

F32 = jnp.float32
BF16 = jnp.bfloat16

D_MODEL = 1024
CHUNK = 64
RET_HEADS = 4
RET_DK = 128
RET_DV = 256
SB_HEADS = 4
SB_D = 128
N_KEYS = 128
N_EXPERTS = N_KEYS * N_KEYS
PEER_HEADS = 8
PEER_HALF = 128
PEER_TOPK = 16
ROPE_BASE = 10000.0
EPS = 1e-6
RET_QK_W = RET_HEADS * RET_DK
RET_V_W = RET_HEADS * RET_DV
SB_W = SB_HEADS * SB_D
IN_W = 2 * RET_QK_W + 2 * RET_V_W + 3 * SB_W + 2 * D_MODEL

VMEM_LIMIT_BYTES = 56 * 1024 * 1024
NEG_BIG = -1e30


def _pick_tile(n, candidates):
    for c in candidates:
        if n % c == 0:
            return c
    raise ValueError(f"no tile in {candidates} divides {n}")


def _params(*sem):
    return pltpu.CompilerParams(dimension_semantics=sem, vmem_limit_bytes=VMEM_LIMIT_BYTES)


def _in_proj_kernel(x_ref, g_ref, w_ref, cos_ref, sin_ref,
                    rq_ref, rk_ref, rv_ref, rg_ref, sq_ref, sk_ref, sv_ref, skb_ref, svb_ref, gt_ref):
    x = x_ref[...]
    h = (x * lax.rsqrt(jnp.mean(x * x, axis=-1, keepdims=True) + EPS) * g_ref[...]).astype(BF16)

    def proj(lo, width):
        return jnp.dot(h, w_ref[:, lo:lo + width], preferred_element_type=F32)

    c = cos_ref[...]
    s = sin_ref[...]

    def rotary_store(p, out_ref, scale):
        for hh in range(RET_HEADS):
            ph = p[:, hh * RET_DK:(hh + 1) * RET_DK]
            r = ph * c + pltpu.roll(ph, RET_DK // 2, 1) * s
            if scale != 1.0:
                r = r * scale
            out_ref[:, hh * RET_DK:(hh + 1) * RET_DK] = r.astype(BF16)

    rotary_store(proj(0, RET_QK_W), rq_ref, 1.0)
    rotary_store(proj(RET_QK_W, RET_QK_W), rk_ref, RET_DK ** -0.5)
    off = 2 * RET_QK_W
    for j in range(RET_V_W // 512):
        rv_ref[:, j * 512:(j + 1) * 512] = proj(off + j * 512, 512).astype(BF16)
    off += RET_V_W
    for j in range(RET_V_W // 512):
        g = proj(off + j * 512, 512)
        rg_ref[:, j * 512:(j + 1) * 512] = (g / (1.0 + jnp.exp(-g))).astype(BF16)
    off += RET_V_W
    sq_ref[...] = (proj(off, SB_W) * (SB_D ** -0.5 * LOG2E)).astype(BF16)
    off += SB_W
    tm = x_ref.shape[0]

    def store_rows_by_head(out_ref, val):
        for hh in range(SB_HEADS):
            out_ref[pl.ds(hh, tm, stride=SB_HEADS), :] = val[:, hh * SB_D:(hh + 1) * SB_D]

    k = proj(off, SB_W)
    store_rows_by_head(sk_ref, k)
    skb_ref[...] = k.astype(BF16)
    off += SB_W
    v = proj(off, SB_W)
    store_rows_by_head(sv_ref, v)
    svb_ref[...] = v.astype(BF16)
    off += SB_W
    for j in range(2 * D_MODEL // 512):
        g = proj(off + j * 512, 512)
        gt_ref[:, j * 512:(j + 1) * 512] = (1.0 / (1.0 + jnp.exp(-g))).astype(BF16)


def _in_proj(x2d, norm_w, w_in_bf, cos_tab, sin_tab):
    n = x2d.shape[0]
    tm = _pick_tile(n, (512, 256, 128, 64))
    assert cos_tab.shape[0] % tm == 0
    npos = cos_tab.shape[0] // tm
    row = lambda i: (i, 0)
    const = lambda i: (0, 0)
    pos = lambda i: (i % npos, 0)
    outs = [(1, RET_QK_W, BF16), (1, RET_QK_W, BF16), (1, RET_V_W, BF16), (1, RET_V_W, BF16), (1, SB_W, BF16),
            (SB_HEADS, SB_D, F32), (SB_HEADS, SB_D, F32), (1, SB_W, BF16), (1, SB_W, BF16),
            (1, 2 * D_MODEL, BF16)]
    return pl.pallas_call(
        _in_proj_kernel,
        grid=(n // tm,),
        in_specs=[pl.BlockSpec((tm, D_MODEL), row),
                  pl.BlockSpec((1, D_MODEL), const),
                  pl.BlockSpec((D_MODEL, IN_W), const),
                  pl.BlockSpec((tm, RET_DK), pos),
                  pl.BlockSpec((tm, RET_DK), pos)],
        out_specs=[pl.BlockSpec((tm * r, w), row) for r, w, _ in outs],
        out_shape=[jax.ShapeDtypeStruct((n * r, w), dt) for r, w, dt in outs],
        compiler_params=_params("parallel"),
        name="in_proj",
    )(x2d, norm_w.reshape(1, D_MODEL), w_in_bf, cos_tab, sin_tab)


def _rotary_tables(pos):
    inv = ROPE_BASE ** (-jnp.arange(0, RET_DK, 2, dtype=F32) / RET_DK)
    ang = pos.astype(F32)[:, None] * inv[None, :]
    cos = jnp.cos(ang)
    sin = jnp.sin(ang)
    return jnp.concatenate([cos, cos], axis=1), jnp.concatenate([-sin, sin], axis=1)


def _retention_tables(blk):
    log_g = np.log(1.0 - 2.0 ** (-5.0 - np.arange(RET_HEADS, dtype=np.float64)))
    i = np.arange(blk, dtype=np.float64)
    dist = np.abs(i[:, None] - i[None, :])
    visible = (np.arange(blk)[None, :] // CHUNK) <= (np.arange(blk)[:, None] // CHUNK)
    dmat = np.where(visible[None], np.exp(dist[None] * log_g[:, None, None]), 0.0)
    inter = np.exp((i[None, :] + 1.0) * log_g[:, None])
    kdec = np.exp((blk - 1.0 - i)[None, :] * log_g[:, None])
    inter = np.broadcast_to(inter[:, :, None], (RET_HEADS, blk, RET_DV))
    kdec = np.broadcast_to(kdec[:, :, None], (RET_HEADS, blk, RET_DK))
    sdec = tuple(float(np.exp(blk * g)) for g in log_g)
    return (jnp.asarray(dmat, F32), jnp.asarray(inter, F32), jnp.asarray(kdec, F32)), sdec


def _retention_kernel(sdec, q_ref, k_ref, v_ref, gate_ref, s0_ref, dmat_ref, inter_ref, kdec_ref, gn_ref,
                      r_ref, sfin_ref, state_ref):
    t = pl.program_id(1)

    @pl.when(t == 0)
    def _():
        state_ref[...] = s0_ref[0]

    for hh in range(RET_HEADS):
        q = q_ref[:, hh * RET_DK:(hh + 1) * RET_DK]
        k = k_ref[:, hh * RET_DK:(hh + 1) * RET_DK]
        v = v_ref[:, hh * RET_DV:(hh + 1) * RET_DV]
        state = state_ref[hh]
        sc = lax.dot_general(q, k, (((1,), (1,)), ((), ())), preferred_element_type=F32)
        p = (sc * dmat_ref[hh]).astype(BF16)
        o = jnp.dot(p, v, preferred_element_type=F32)
        o = o + jnp.dot(q, state.astype(BF16), preferred_element_type=F32) * inter_ref[hh]
        kd = (k.astype(F32) * kdec_ref[hh]).astype(BF16)
        upd = lax.dot_general(kd, v, (((0,), (0,)), ((), ())), preferred_element_type=F32)
        state_ref[hh] = sdec[hh] * state + upd
        mu = jnp.mean(o, axis=-1, keepdims=True)
        cen = o - mu
        var = jnp.mean(cen * cen, axis=-1, keepdims=True)
        r = cen * lax.rsqrt(var + EPS) * gn_ref[:, hh * RET_DV:(hh + 1) * RET_DV]
        r = r * gate_ref[:, hh * RET_DV:(hh + 1) * RET_DV].astype(F32)
        r_ref[:, hh * RET_DV:(hh + 1) * RET_DV] = r.astype(BF16)

    @pl.when(t == pl.num_programs(1) - 1)
    def _():
        sfin_ref[0] = state_ref[...]


def _retention(rq, rk, rv, gate, s0, gn_w, batch, seq):
    blk = _pick_tile(seq, (256, 128, 64))
    nt = seq // blk
    tabs, sdec = _retention_tables(blk)
    row = lambda b, t: (b * nt + t, 0)
    per_b = lambda b, t: (b, 0, 0, 0)
    const3 = lambda b, t: (0, 0, 0)
    return pl.pallas_call(
        functools.partial(_retention_kernel, sdec),
        grid=(batch, nt),
        in_specs=[pl.BlockSpec((blk, RET_QK_W), row),
                  pl.BlockSpec((blk, RET_QK_W), row),
                  pl.BlockSpec((blk, RET_V_W), row),
                  pl.BlockSpec((blk, RET_V_W), row),
                  pl.BlockSpec((1, RET_HEADS, RET_DK, RET_DV), per_b),
                  pl.BlockSpec((RET_HEADS, blk, blk), const3),
                  pl.BlockSpec((RET_HEADS, blk, RET_DV), const3),
                  pl.BlockSpec((RET_HEADS, blk, RET_DK), const3),
                  pl.BlockSpec((1, RET_V_W), lambda b, t: (0, 0))],
        out_specs=[pl.BlockSpec((blk, RET_V_W), row),
                   pl.BlockSpec((1, RET_HEADS, RET_DK, RET_DV), per_b)],
        out_shape=[jax.ShapeDtypeStruct((batch * seq, RET_V_W), BF16),
                   jax.ShapeDtypeStruct((batch, RET_HEADS, RET_DK, RET_DV), F32)],
        scratch_shapes=[pltpu.VMEM((RET_HEADS, RET_DK, RET_DV), F32)],
        compiler_params=_params("parallel", "arbitrary"),
        name="retention",
    )(rq, rk, rv, gate, s0, *tabs, gn_w.reshape(1, RET_V_W))


SB_BLOCK = 256
SB_UNROLL = 8
LOG2E = math.log2(math.e)
SOFTPLUS_CLAMP = 30.0 * LOG2E


def _softplus2(z):
    return jnp.maximum(jnp.log(1.0 + jnp.exp2(jnp.minimum(z, SOFTPLUS_CLAMP))) * LOG2E, z)


def _suffix_matrix(n):
    return jnp.asarray(np.tril(np.ones((n, n), np.float32), -1), BF16)


def _sb_block(q, kblk, vblk, umat, carry, mask):
    z = lax.dot_general(q, kblk, (((1,), (1,)), ((), ())), preferred_element_type=F32)
    sp = _softplus2(z)
    spm = sp if mask is None else jnp.where(mask, sp, 0.0)
    after = jnp.dot(spm.astype(BF16), umat, preferred_element_type=F32)
    w = jnp.exp2(z - sp - after - carry)
    if mask is not None:
        w = jnp.where(mask, w, 0.0)
    out = jnp.dot(w.astype(BF16), vblk, preferred_element_type=F32)
    return out, carry + after[:, 0:1] + spm[:, 0:1]


def _causal_mask(n):
    r = lax.broadcasted_iota(jnp.int32, (n, n), 0)
    c = lax.broadcasted_iota(jnp.int32, (n, n), 1)
    return c < r


def _sb_prompt_kernel(q_ref, k_ref, v_ref, u_ref, o_ref, acc_ref, carry_ref):
    qi = pl.program_id(2)
    blk = q_ref.shape[0]
    q = q_ref[...]
    umat = u_ref[...]

    def sweep(first, n, diagonal):
        s = pl.multiple_of((first - n + 1) * blk, blk)
        keys = k_ref[pl.ds(s, n * blk), :]
        vals = v_ref[pl.ds(s, n * blk), :]
        z = lax.dot_general(q, keys, (((1,), (1,)), ((), ())), preferred_element_type=F32)
        sp = _softplus2(z)
        fail = [sp[:, u * blk:(u + 1) * blk] for u in range(n)]
        if diagonal:
            mask = _causal_mask(blk)
            fail[-1] = jnp.where(mask, fail[-1], 0.0)
        after = jnp.dot(jnp.concatenate(fail, axis=0).astype(BF16), umat, preferred_element_type=F32)
        carry = jnp.zeros((blk, 1), F32) if diagonal else carry_ref[...]
        offsets = [None] * n
        for u in reversed(range(n)):
            offsets[u] = carry
            carry = carry + after[u * blk:(u + 1) * blk, 0:1] + fail[u][:, 0:1]
        later = jnp.concatenate([after[u * blk:(u + 1) * blk, :] + offsets[u] for u in range(n)], axis=1)
        w = jnp.exp2(z - sp - later)
        if diagonal:
            near = jnp.where(mask, w[:, (n - 1) * blk:], 0.0)
            w = near if n == 1 else jnp.concatenate([w[:, :(n - 1) * blk], near], axis=1)
        out = jnp.dot(w.astype(BF16), vals, preferred_element_type=F32)
        if diagonal:
            acc_ref[...] = out
        else:
            acc_ref[...] += out
        carry_ref[...] = carry

    lead = qi % SB_UNROLL + 1
    for n in range(1, SB_UNROLL + 1):
        @pl.when(lead == n)
        def _():
            sweep(qi, n, True)

    def main_body(i, _):
        sweep(qi - lead - i * SB_UNROLL, SB_UNROLL, False)
        return 0

    lax.fori_loop(0, qi // SB_UNROLL, main_body, 0)
    o_ref[...] = acc_ref[...].astype(BF16)


def _sb_prompt(sq, skb, svb, batch, seq):
    blk = _pick_tile(seq, (SB_BLOCK, 128, 64))
    nq = seq // blk
    qmap = lambda b, h, i: (b * nq + i, h)
    kvmap = lambda b, h, i: (b, h)
    return pl.pallas_call(
        _sb_prompt_kernel,
        grid=(batch, SB_HEADS, nq),
        in_specs=[pl.BlockSpec((blk, SB_D), qmap),
                  pl.BlockSpec((seq, SB_D), kvmap),
                  pl.BlockSpec((seq, SB_D), kvmap),
                  pl.BlockSpec((blk, blk), lambda b, h, i: (0, 0))],
        out_specs=pl.BlockSpec((blk, SB_D), qmap),
        out_shape=jax.ShapeDtypeStruct((batch * seq, SB_W), BF16),
        scratch_shapes=[pltpu.VMEM((blk, SB_D), F32), pltpu.VMEM((blk, 1), F32)],
        compiler_params=_params("parallel", "parallel", "arbitrary"),
        name="sb_prompt",
    )(sq, skb, svb, _suffix_matrix(blk))


def _sb_sample_kernel(pblk, q_ref, k_ref, v_ref, pk_ref, pv_ref, u_ref, o_ref):
    t = q_ref.shape[0]
    past = pk_ref.shape[0]
    q = q_ref[...]
    acc, carry = _sb_block(q, k_ref[...], v_ref[...], u_ref[:t, :t], jnp.zeros((t, 1), F32), _causal_mask(t))
    for j in reversed(range(past // pblk)):
        kb = pk_ref[j * pblk:(j + 1) * pblk, :].astype(BF16)
        vb = pv_ref[j * pblk:(j + 1) * pblk, :].astype(BF16)
        out, carry = _sb_block(q, kb, vb, u_ref[:pblk, :pblk], carry, None)
        acc = acc + out
    o_ref[...] = acc.astype(BF16)


def _sb_sample(sq, skb, svb, past_k, past_v, batch, seq, past):
    pblk = _pick_tile(past, (SB_BLOCK, 128, 64))
    ublk = max(pblk, seq)
    cur = lambda b, h: (b, h)
    return pl.pallas_call(
        functools.partial(_sb_sample_kernel, pblk),
        grid=(batch, SB_HEADS),
        in_specs=[pl.BlockSpec((seq, SB_D), cur),
                  pl.BlockSpec((seq, SB_D), cur),
                  pl.BlockSpec((seq, SB_D), cur),
                  pl.BlockSpec((past, SB_D), cur),
                  pl.BlockSpec((past, SB_D), cur),
                  pl.BlockSpec((ublk, ublk), lambda b, h: (0, 0))],
        out_specs=pl.BlockSpec((seq, SB_D), cur),
        out_shape=jax.ShapeDtypeStruct((batch * seq, SB_W), BF16),
        compiler_params=_params("parallel", "parallel"),
        name="sb_sample",
    )(sq, skb, svb, past_k, past_v, _suffix_matrix(ublk))


def _merge_kernel(x_ref, r_ref, sb_ref, gt_ref, wr_ref, ws_ref, wo_ref, g2_ref, x1_ref, h2t_ref):
    rb = jnp.dot(r_ref[...], wr_ref[...], preferred_element_type=F32)
    sb = jnp.dot(sb_ref[...], ws_ref[...], preferred_element_type=F32)
    mixed = gt_ref[:, :D_MODEL].astype(F32) * rb + gt_ref[:, D_MODEL:].astype(F32) * sb
    x1 = x_ref[...] + jnp.dot(mixed.astype(BF16), wo_ref[...], preferred_element_type=F32)
    x1_ref[...] = x1
    h2 = x1 * lax.rsqrt(jnp.mean(x1 * x1, axis=-1, keepdims=True) + EPS) * g2_ref[...]
    h2t_ref[...] = h2.T.astype(BF16)


def _merge(x2d, r, sbo, gates, w_ret_o, w_sb_o, w_out, norm2_w):
    n = x2d.shape[0]
    tm = _pick_tile(n, (512, 256, 128))
    row = lambda i: (i, 0)
    const = lambda i: (0, 0)
    return pl.pallas_call(
        _merge_kernel,
        grid=(n // tm,),
        in_specs=[pl.BlockSpec((tm, D_MODEL), row),
                  pl.BlockSpec((tm, RET_V_W), row),
                  pl.BlockSpec((tm, SB_W), row),
                  pl.BlockSpec((tm, 2 * D_MODEL), row),
                  pl.BlockSpec((RET_V_W, D_MODEL), const),
                  pl.BlockSpec((SB_W, D_MODEL), const),
                  pl.BlockSpec((D_MODEL, D_MODEL), const),
                  pl.BlockSpec((1, D_MODEL), const)],
        out_specs=[pl.BlockSpec((tm, D_MODEL), row),
                   pl.BlockSpec((D_MODEL, tm), lambda i: (0, i))],
        out_shape=[jax.ShapeDtypeStruct((n, D_MODEL), F32),
                   jax.ShapeDtypeStruct((D_MODEL, n), BF16)],
        compiler_params=_params("parallel"),
        name="merge",
    )(x2d, r, sbo, gates, w_ret_o, w_sb_o, w_out, norm2_w.reshape(1, D_MODEL))


PEER_SLAB = 1024


def _sort_desc(a):
    a = list(a)
    n = len(a)
    k = 2
    while k <= n:
        j = k // 2
        while j >= 1:
            for i in range(n):
                l = i ^ j
                if l > i:
                    hi = jnp.maximum(a[i], a[l])
                    lo = jnp.minimum(a[i], a[l])
                    a[i], a[l] = (hi, lo) if (i & k) == 0 else (lo, hi)
            j //= 2
        k *= 2
    return a


def _merge_bitonic_desc(a):
    a = list(a)
    n = len(a)
    j = n // 2
    while j >= 1:
        for i in range(n):
            l = i ^ j
            if l > i:
                a[i], a[l] = jnp.maximum(a[i], a[l]), jnp.minimum(a[i], a[l])
        j //= 2
    return a


def _top_merge(a, b):
    n = len(a)
    return _merge_bitonic_desc([jnp.maximum(a[i], b[n - 1 - i]) for i in range(n)])


def _top16_rows(s):
    groups = s.shape[0] // 8
    a = _sort_desc([s[8 * j:8 * j + 8, :] for j in range(groups)])
    for shift in (4, 2, 1):
        a = _top_merge(a, [pltpu.roll(x, shift, 0) for x in a])
    return a


def _gelu_tanh(x):
    c0 = math.sqrt(2.0 / math.pi)
    half = 0.5 * x
    return half + half * jnp.tanh(x * (c0 + (c0 * 0.044715) * (x * x)))


def _twin_bf16_words(x):
    hi = pltpu.bitcast(x.astype(BF16).astype(F32), jnp.uint32)
    return hi | (hi >> 16)


def _lookup_desc(keys, vals, default, s):
    rows, tn = s.shape
    s3 = s.reshape(rows // 8, 8, tn)
    out = jnp.full_like(s3, default)
    for b in reversed(range(len(keys))):
        v = vals[b] if isinstance(vals[b], float) else vals[b][None]
        out = jnp.where(s3 == keys[b][None], v, out)
    return out.reshape(rows, tn)


def _peer_kernel(n_slabs, h2t_ref, x1_ref, wq_ref, keys_ref, u_ref, vt_ref, gf_ref, y_ref,
                 s1_ref, s2_ref, cnt_ref, e1_ref, rank2_ref, e2_ref, coef0_ref, coef1_ref, gate0_ref, gate1_ref,
                 acc_ref):
    j = pl.program_id(1)
    ns = n_slabs
    coef_refs = (coef0_ref, coef1_ref)
    gate_refs = (gate0_ref, gate1_ref)
    tn = h2t_ref.shape[1]
    ht = h2t_ref[...]

    @pl.when(j == 0)
    def _():
        qt = jnp.dot(wq_ref[...], ht, preferred_element_type=F32).astype(BF16)
        for p, dst in ((0, s1_ref), (1, s2_ref)):
            for hh in range(PEER_HEADS):
                r0 = (p * PEER_HEADS + hh) * PEER_HALF
                dst[hh] = jnp.dot(keys_ref[p, hh], qt[r0:r0 + PEER_HALF, :], preferred_element_type=F32)

        lw = 128

        def selection_tables(ci, _):
            lanes = pl.ds(pl.multiple_of(ci * lw, lw), lw)
            sub = lax.broadcasted_iota(jnp.int32, (8, lw), 0)
            v1 = [jnp.zeros((8, lw), F32) for _ in range(PEER_TOPK)]
            v2 = [jnp.zeros((8, lw), F32) for _ in range(PEER_TOPK)]
            ranks = [float(b) for b in range(PEER_TOPK)]
            for hh in range(PEER_HEADS):
                top = _top16_rows(s1_ref[hh, :, lanes])
                v1 = [jnp.where(sub == hh, top[a], v1[a]) for a in range(PEER_TOPK)]
                s2 = s2_ref[hh, :, lanes]
                top = _top16_rows(s2)
                v2 = [jnp.where(sub == hh, top[a], v2[a]) for a in range(PEER_TOPK)]
                rank2_ref[hh, :, lanes] = _lookup_desc(top, ranks, float(PEER_TOPK), s2).astype(BF16)
                e2 = jnp.exp(s2.reshape(N_KEYS // 8, 8, lw) - top[0][None])
                e2_ref[hh, :, lanes] = e2.reshape(N_KEYS, lw).astype(BF16)
            cand_rows = [[v1[a] + v2[b] for b in range(PEER_TOPK // (a + 1))] for a in range(PEER_TOPK)]
            cands = [c for row in cand_rows for c in row]
            pad = [jnp.full((8, lw), NEG_BIG, F32)] * (-len(cands) % PEER_TOPK)
            padded = cands + pad
            lists = [_sort_desc(padded[i:i + PEER_TOPK]) for i in range(0, len(padded), PEER_TOPK)]
            while len(lists) > 1:
                lists = [_top_merge(lists[i], lists[i + 1]) for i in range(0, len(lists), 2)]
            tau = lists[0][PEER_TOPK - 1]
            m = cands[0]
            z = jnp.zeros((8, lw), F32)
            for cnd in cands:
                z = z + jnp.where(cnd >= tau, jnp.exp(cnd - m), 0.0)
            shift = v1[0] + jnp.log(z)
            cnt_rows = []
            for row in cand_rows:
                n_sel = jnp.zeros((8, lw), F32)
                for cnd in row:
                    n_sel = n_sel + jnp.where(cnd >= tau, 1.0, 0.0)
                cnt_rows.append(n_sel)
            for hh in range(PEER_HEADS):
                s1 = s1_ref[hh, :, lanes]
                head_row = lambda x: jnp.broadcast_to(x[hh:hh + 1, :], (8, lw))
                cnt = _lookup_desc([head_row(v) for v in v1], [head_row(c) for c in cnt_rows], 0.0, s1)
                cnt_ref[hh, :, lanes] = _twin_bf16_words(cnt)
                e1_ref[hh, :, lanes] = _twin_bf16_words(jnp.exp(s1 - shift[hh:hh + 1, :]))
            return 0

        lax.fori_loop(0, tn // lw, selection_tables, 0)
        acc_ref[...] = jnp.zeros_like(acc_ref)
        coef_refs[1][...] = jnp.zeros_like(coef_refs[1])

    n_i1 = u_ref.shape[0] // N_KEYS
    pack = 16

    def gate_rows(slab):
        out = []
        for ii in range(n_i1):
            i1 = jnp.minimum(slab * n_i1 + ii, N_KEYS - 1)
            out.append([(cnt_ref[hh, pl.ds(i1, 1), :], e1_ref[hh, pl.ds(i1, 1), :])
                        for hh in range(PEER_HEADS)])
        return out

    def gate_group(dst_ref, table_rows, ii):
        as_tile = lambda row: pltpu.bitcast(jnp.broadcast_to(row, (pack // 2, tn)), BF16)
        cnt_b = [as_tile(c) for c, _ in table_rows[ii]]
        e1_b = [as_tile(e) for _, e in table_rows[ii]]
        for c in range(N_KEYS // pack):
            rows = slice(c * pack, (c + 1) * pack)
            g = jnp.zeros((pack, tn), BF16)
            for hh in range(PEER_HEADS):
                w = e2_ref[hh, rows, :] * e1_b[hh]
                g = g + jnp.where(rank2_ref[hh, rows, :] < cnt_b[hh], w, jnp.zeros_like(w))
            r0 = ii * N_KEYS + c * pack
            dst_ref[r0:r0 + pack, :] = g

    @pl.when(j == 0)
    def _():
        rows0 = gate_rows(0)
        for ii in range(n_i1):
            gate_group(gate_refs[0], rows0, ii)

    def add_previous_slab(prev_ref):
        acc_ref[...] += jnp.dot(vt_ref[...], prev_ref[...], preferred_element_type=F32)

    def build_slab(parity):
        cur_ref, prev_ref = coef_refs[parity], coef_refs[1 - parity]
        gate_ref, next_gate_ref = gate_refs[parity], gate_refs[1 - parity]
        next_rows = gate_rows(j + 1)
        n_split = 2 if tn % 512 == 0 else 1
        wn = tn // n_split
        m_split = 2
        em = u_ref.shape[0] // m_split
        dm = D_MODEL // m_split
        gm = em // N_KEYS
        assert n_i1 == m_split * gm

        dots = {}

        def expert_dot_quadrant(mi, ni):
            dots[mi, ni] = jnp.dot(u_ref[mi * em:(mi + 1) * em, :], h2t_ref[:, ni * wn:(ni + 1) * wn],
                                   preferred_element_type=F32)

        def previous_slab_quadrant(mi, ni):
            blk = (slice(mi * dm, (mi + 1) * dm), slice(ni * wn, (ni + 1) * wn))
            acc_ref[blk] += jnp.dot(vt_ref[mi * dm:(mi + 1) * dm, :], prev_ref[:, ni * wn:(ni + 1) * wn],
                                    preferred_element_type=F32)

        def coef_group(ii):
            mi, local = divmod(ii, gm)
            for c in range(N_KEYS // pack):
                for ni in range(n_split):
                    lr = local * N_KEYS + c * pack
                    act = _gelu_tanh(dots[mi, ni][lr:lr + pack, :].astype(BF16))
                    r0 = ii * N_KEYS + c * pack
                    cols = slice(ni * wn, (ni + 1) * wn)
                    cur_ref[r0:r0 + pack, cols] = gate_ref[r0:r0 + pack, cols] * act

        quads = [(mi, ni) for mi in range(m_split) for ni in range(n_split)]
        matmuls = ([functools.partial(expert_dot_quadrant, *q) for q in quads]
                   + [functools.partial(previous_slab_quadrant, *q) for q in quads])
        tasks = [[] for _ in matmuls]
        first_coef_slot = n_split
        for ii in range(n_i1):
            slot = min(first_coef_slot + ii // 2, len(tasks) - 1)
            slot = max(slot, (ii // gm + 1) * n_split)
            tasks[min(slot, len(tasks) - 1)].append(functools.partial(coef_group, ii))
        for ii in range(n_i1):
            tasks[ii * len(tasks) // n_i1].append(functools.partial(gate_group, next_gate_ref, next_rows, ii))
        for mm, work in zip(matmuls, tasks):
            mm()
            for w in work:
                w()

    for parity in range(2):
        @pl.when(jnp.logical_and(j < ns, j % 2 == parity))
        def _():
            build_slab(parity)

    @pl.when(j == ns)
    def _():
        add_previous_slab(coef_refs[(n_slabs - 1) % 2])
        x2 = x1_ref[...] + acc_ref[...].T
        y = x2 * lax.rsqrt(jnp.mean(x2 * x2, axis=-1, keepdims=True) + EPS) * gf_ref[...]
        y_ref[...] = y


def _peer(h2t, x1, wq_t, keys_bf, u_bf, vt_bf, norm_f_w):
    n = x1.shape[0]
    tn = _pick_tile(n, (512, 256, 128))
    ns = N_EXPERTS // PEER_SLAB
    table = (PEER_HEADS, N_KEYS, tn)
    return pl.pallas_call(
        functools.partial(_peer_kernel, ns),
        grid=(n // tn, ns + 1),
        in_specs=[pl.BlockSpec((D_MODEL, tn), lambda i, j: (0, i)),
                  pl.BlockSpec((tn, D_MODEL), lambda i, j: (i, 0)),
                  pl.BlockSpec((2 * PEER_HEADS * PEER_HALF, D_MODEL), lambda i, j: (0, 0)),
                  pl.BlockSpec((2, PEER_HEADS, N_KEYS, PEER_HALF), lambda i, j: (0, 0, 0, 0)),
                  pl.BlockSpec((PEER_SLAB, D_MODEL), lambda i, j: (jnp.minimum(j, ns - 1), 0)),
                  pl.BlockSpec((D_MODEL, PEER_SLAB), lambda i, j: (0, jnp.maximum(j - 1, 0))),
                  pl.BlockSpec((1, D_MODEL), lambda i, j: (0, 0))],
        out_specs=pl.BlockSpec((tn, D_MODEL), lambda i, j: (i, 0)),
        out_shape=jax.ShapeDtypeStruct((n, D_MODEL), F32),
        scratch_shapes=[pltpu.VMEM(table, F32),
                        pltpu.VMEM(table, F32),
                        pltpu.VMEM(table, jnp.uint32),
                        pltpu.VMEM(table, jnp.uint32),
                        pltpu.VMEM(table, BF16),
                        pltpu.VMEM(table, BF16),
                        pltpu.VMEM((PEER_SLAB, tn), BF16),
                        pltpu.VMEM((PEER_SLAB, tn), BF16),
                        pltpu.VMEM((PEER_SLAB, tn), BF16),
                        pltpu.VMEM((PEER_SLAB, tn), BF16),
                        pltpu.VMEM((D_MODEL, tn), F32)],
        compiler_params=_params("parallel", "arbitrary"),
        name="peer",
    )(h2t, x1, wq_t, keys_bf, u_bf, vt_bf, norm_f_w.reshape(1, D_MODEL))


def _layer(x, pos, past_k, past_v, s0, wts):
    batch, seq, _ = x.shape
    n = batch * seq
    x2d = x.reshape(n, D_MODEL)
    cos_tab, sin_tab = _rotary_tables(pos)
    tm = _pick_tile(n, (512, 256, 128, 64))
    if seq < tm:
        cos_tab = jnp.tile(cos_tab, (tm // seq, 1))
        sin_tab = jnp.tile(sin_tab, (tm // seq, 1))
    rq, rk, rv, gate, sq, sk, sv, skb, svb, gates = _in_proj(x2d, wts["norm1_w"], wts["w_in"], cos_tab, sin_tab)
    r, s_fin = _retention(rq, rk, rv, gate, s0, wts["ret_gn_w"], batch, seq)
    if past_k is None:
        sbo = _sb_prompt(sq, skb, svb, batch, seq)
    else:
        sbo = _sb_sample(sq, skb, svb, past_k, past_v, batch, seq, past_k.shape[0] // batch)
    x1, h2t = _merge(x2d, r, sbo, gates, wts["w_ret_o"], wts["w_sb_o"], wts["w_out"], wts["norm2_w"])
    y = _peer(h2t, x1, wts["w_pq_t"], wts["pk_keys"], wts["peer_u"], wts["peer_v_t"], wts["norm_f_w"])
    return (y.reshape(batch, seq, D_MODEL), sk.reshape(batch, seq, SB_HEADS, SB_D),
            sv.reshape(batch, seq, SB_HEADS, SB_D), s_fin)


def kernel(x_prompt, x_sample, cache_sb_k, cache_sb_v, state_ret, norm1_w, w_in, ret_gn_w, w_ret_o, w_sb_o,
           w_out, norm2_w, w_pq, pk_keys, peer_u, peer_v, norm_f_w):
    assert norm1_w.shape[0] == 1, "single-layer model"
    wq_t = w_pq[0].reshape(D_MODEL, PEER_HEADS, 2, PEER_HALF).transpose(2, 1, 3, 0)
    wq_t = wq_t.reshape(2 * PEER_HEADS * PEER_HALF, D_MODEL).astype(BF16)
    wts = dict(
        norm1_w=norm1_w[0], w_in=w_in[0].astype(BF16), ret_gn_w=ret_gn_w[0],
        w_ret_o=w_ret_o[0].astype(BF16), w_sb_o=w_sb_o[0].astype(BF16), w_out=w_out[0].astype(BF16),
        norm2_w=norm2_w[0], w_pq_t=wq_t, pk_keys=pk_keys[0].astype(BF16),
        peer_u=peer_u[0].astype(BF16), peer_v_t=peer_v[0].T.astype(BF16), norm_f_w=norm_f_w)
    bp, tp, _ = x_prompt.shape
    bs, ts, _ = x_sample.shape
    past = cache_sb_k.shape[2]
    yp, kp, vp, sp = _layer(x_prompt, jnp.arange(tp, dtype=jnp.int32), None, None,
                            jnp.zeros((bp, RET_HEADS, RET_DK, RET_DV), F32), wts)
    ys, ks, vs, ss = _layer(x_sample, past + jnp.arange(ts, dtype=jnp.int32),
                            cache_sb_k[0].reshape(bs * past, SB_W), cache_sb_v[0].reshape(bs * past, SB_W),
                            state_ret[0], wts)
    return (yp, ys, kp[None], vp[None], sp[None], ks[None], vs[None], ss[None])
```

```python
import functools
import math

import numpy as np
import jax
import jax.numpy as jnp
from jax import lax
from jax.experimental import pallas as pl
from jax.experimental.pallas import tpu as pltpu

F32 = jnp.float32
BF16 = jnp.bfloat16

D_MODEL = 1024
CHUNK = 64
RET_HEADS = 4
RET_DK = 128
RET_DV = 256
SB_HEADS = 4
SB_D = 128
N_KEYS = 128
N_EXPERTS = N_KEYS * N_KEYS
PEER_HEADS = 8
PEER_HALF = 128
PEER_TOPK = 16
ROPE_BASE = 10000.0
EPS = 1e-6
RET_QK_W = RET_HEADS * RET_DK
RET_V_W = RET_HEADS * RET_DV
SB_W = SB_HEADS * SB_D
IN_W = 2 * RET_QK_W + 2 * RET_V_W + 3 * SB_W + 2 * D_MODEL

VMEM_LIMIT_BYTES = 56 * 1024 * 1024
NEG_BIG = -1e30


def _pick_tile(n, candidates):
    for c in candidates:
        if n % c == 0:
            return c
    raise ValueError(f"no tile in {candidates} divides {n}")


def _params(*sem):
    return pltpu.CompilerParams(dimension_semantics=sem, vmem_limit_bytes=VMEM_LIMIT_BYTES)


def _in_proj_kernel(x_ref, g_ref, w_ref, cos_ref, sin_ref,
                    rq_ref, rk_ref, rv_ref, rg_ref, sq_ref, sk_ref, sv_ref, skb_ref, svb_ref, gt_ref):
    x = x_ref[...]
    h = (x * lax.rsqrt(jnp.mean(x * x, axis=-1, keepdims=True) + EPS) * g_ref[...]).astype(BF16)

    def proj(lo, width):
        return jnp.dot(h, w_ref[:, lo:lo + width], preferred_element_type=F32)

    c = cos_ref[...]
    s = sin_ref[...]

    def rotary_store(p, out_ref, scale):
        for hh in range(RET_HEADS):
            ph = p[:, hh * RET_DK:(hh + 1) * RET_DK]
            r = ph * c + pltpu.roll(ph, RET_DK // 2, 1) * s
            if scale != 1.0:
                r = r * scale
            out_ref[:, hh * RET_DK:(hh + 1) * RET_DK] = r.astype(BF16)

    rotary_store(proj(0, RET_QK_W), rq_ref, 1.0)
    rotary_store(proj(RET_QK_W, RET_QK_W), rk_ref, RET_DK ** -0.5)
    off = 2 * RET_QK_W
    for j in range(RET_V_W // 512):
        rv_ref[:, j * 512:(j + 1) * 512] = proj(off + j * 512, 512).astype(BF16)
    off += RET_V_W
    for j in range(RET_V_W // 512):
        g = proj(off + j * 512, 512)
        rg_ref[:, j * 512:(j + 1) * 512] = (g / (1.0 + jnp.exp(-g))).astype(BF16)
    off += RET_V_W
    sq_ref[...] = (proj(off, SB_W) * (SB_D ** -0.5 * LOG2E)).astype(BF16)
    off += SB_W
    tm = x_ref.shape[0]

    def store_rows_by_head(out_ref, val):
        for hh in range(SB_HEADS):
            out_ref[pl.ds(hh, tm, stride=SB_HEADS), :] = val[:, hh * SB_D:(hh + 1) * SB_D]

    k = proj(off, SB_W)
    store_rows_by_head(sk_ref, k)
    skb_ref[...] = k.astype(BF16)
    off += SB_W
    v = proj(off, SB_W)
    store_rows_by_head(sv_ref, v)
    svb_ref[...] = v.astype(BF16)
    off += SB_W
    for j in range(2 * D_MODEL // 512):
        g = proj(off + j * 512, 512)
        gt_ref[:, j * 512:(j + 1) * 512] = (1.0 / (1.0 + jnp.exp(-g))).astype(BF16)


def _in_proj(x2d, norm_w, w_in_bf, cos_tab, sin_tab):
    n = x2d.shape[0]
    tm = _pick_tile(n, (512, 256, 128, 64))
    assert cos_tab.shape[0] % tm == 0
    npos = cos_tab.shape[0] // tm
    row = lambda i: (i, 0)
    const = lambda i: (0, 0)
    pos = lambda i: (i % npos, 0)
    outs = [(1, RET_QK_W, BF16), (1, RET_QK_W, BF16), (1, RET_V_W, BF16), (1, RET_V_W, BF16), (1, SB_W, BF16),
            (SB_HEADS, SB_D, F32), (SB_HEADS, SB_D, F32), (1, SB_W, BF16), (1, SB_W, BF16),
            (1, 2 * D_MODEL, BF16)]
    return pl.pallas_call(
        _in_proj_kernel,
        grid=(n // tm,),
        in_specs=[pl.BlockSpec((tm, D_MODEL), row),
                  pl.BlockSpec((1, D_MODEL), const),
                  pl.BlockSpec((D_MODEL, IN_W), const),
                  pl.BlockSpec((tm, RET_DK), pos),
                  pl.BlockSpec((tm, RET_DK), pos)],
        out_specs=[pl.BlockSpec((tm * r, w), row) for r, w, _ in outs],
        out_shape=[jax.ShapeDtypeStruct((n * r, w), dt) for r, w, dt in outs],
        compiler_params=_params("parallel"),
        name="in_proj",
    )(x2d, norm_w.reshape(1, D_MODEL), w_in_bf, cos_tab, sin_tab)


def _rotary_tables(pos):
    inv = ROPE_BASE ** (-jnp.arange(0, RET_DK, 2, dtype=F32) / RET_DK)
    ang = pos.astype(F32)[:, None] * inv[None, :]
    cos = jnp.cos(ang)
    sin = jnp.sin(ang)
    return jnp.concatenate([cos, cos], axis=1), jnp.concatenate([-sin, sin], axis=1)


def _retention_tables(blk):
    log_g = np.log(1.0 - 2.0 ** (-5.0 - np.arange(RET_HEADS, dtype=np.float64)))
    i = np.arange(blk, dtype=np.float64)
    dist = np.abs(i[:, None] - i[None, :])
    visible = (np.arange(blk)[None, :] // CHUNK) <= (np.arange(blk)[:, None] // CHUNK)
    dmat = np.where(visible[None], np.exp(dist[None] * log_g[:, None, None]), 0.0)
    inter = np.exp((i[None, :] + 1.0) * log_g[:, None])
    kdec = np.exp((blk - 1.0 - i)[None, :] * log_g[:, None])
    inter = np.broadcast_to(inter[:, :, None], (RET_HEADS, blk, RET_DV))
    kdec = np.broadcast_to(kdec[:, :, None], (RET_HEADS, blk, RET_DK))
    sdec = tuple(float(np.exp(blk * g)) for g in log_g)
    return (jnp.asarray(dmat, F32), jnp.asarray(inter, F32), jnp.asarray(kdec, F32)), sdec


def _retention_kernel(sdec, q_ref, k_ref, v_ref, gate_ref, s0_ref, dmat_ref, inter_ref, kdec_ref, gn_ref,
                      r_ref, sfin_ref, state_ref):
    t = pl.program_id(1)

    @pl.when(t == 0)
    def _():
        state_ref[...] = s0_ref[0]

    for hh in range(RET_HEADS):
        q = q_ref[:, hh * RET_DK:(hh + 1) * RET_DK]
        k = k_ref[:, hh * RET_DK:(hh + 1) * RET_DK]
        v = v_ref[:, hh * RET_DV:(hh + 1) * RET_DV]
        state = state_ref[hh]
        sc = lax.dot_general(q, k, (((1,), (1,)), ((), ())), preferred_element_type=F32)
        p = (sc * dmat_ref[hh]).astype(BF16)
        o = jnp.dot(p, v, preferred_element_type=F32)
        o = o + jnp.dot(q, state.astype(BF16), preferred_element_type=F32) * inter_ref[hh]
        kd = (k.astype(F32) * kdec_ref[hh]).astype(BF16)
        upd = lax.dot_general(kd, v, (((0,), (0,)), ((), ())), preferred_element_type=F32)
        state_ref[hh] = sdec[hh] * state + upd
        mu = jnp.mean(o, axis=-1, keepdims=True)
        cen = o - mu
        var = jnp.mean(cen * cen, axis=-1, keepdims=True)
        r = cen * lax.rsqrt(var + EPS) * gn_ref[:, hh * RET_DV:(hh + 1) * RET_DV]
        r = r * gate_ref[:, hh * RET_DV:(hh + 1) * RET_DV].astype(F32)
        r_ref[:, hh * RET_DV:(hh + 1) * RET_DV] = r.astype(BF16)

    @pl.when(t == pl.num_programs(1) - 1)
    def _():
        sfin_ref[0] = state_ref[...]


def _retention(rq, rk, rv, gate, s0, gn_w, batch, seq):
    blk = _pick_tile(seq, (256, 128, 64))
    nt = seq // blk
    tabs, sdec = _retention_tables(blk)
    row = lambda b, t: (b * nt + t, 0)
    per_b = lambda b, t: (b, 0, 0, 0)
    const3 = lambda b, t: (0, 0, 0)
    return pl.pallas_call(
        functools.partial(_retention_kernel, sdec),
        grid=(batch, nt),
        in_specs=[pl.BlockSpec((blk, RET_QK_W), row),
                  pl.BlockSpec((blk, RET_QK_W), row),
                  pl.BlockSpec((blk, RET_V_W), row),
                  pl.BlockSpec((blk, RET_V_W), row),
                  pl.BlockSpec((1, RET_HEADS, RET_DK, RET_DV), per_b),
                  pl.BlockSpec((RET_HEADS, blk, blk), const3),
                  pl.BlockSpec((RET_HEADS, blk, RET_DV), const3),
                  pl.BlockSpec((RET_HEADS, blk, RET_DK), const3),
                  pl.BlockSpec((1, RET_V_W), lambda b, t: (0, 0))],
        out_specs=[pl.BlockSpec((blk, RET_V_W), row),
                   pl.BlockSpec((1, RET_HEADS, RET_DK, RET_DV), per_b)],
        out_shape=[jax.ShapeDtypeStruct((batch * seq, RET_V_W), BF16),
                   jax.ShapeDtypeStruct((batch, RET_HEADS, RET_DK, RET_DV), F32)],
        scratch_shapes=[pltpu.VMEM((RET_HEADS, RET_DK, RET_DV), F32)],
        compiler_params=_params("parallel", "arbitrary"),
        name="retention",
    )(rq, rk, rv, gate, s0, *tabs, gn_w.reshape(1, RET_V_W))


SB_BLOCK = 256
SB_QUERY_BLOCK = 512
SB_UNROLL = 8
LOG2E = math.log2(math.e)
SOFTPLUS_CLAMP = 30.0 * LOG2E


def _softplus2(z):
    return jnp.maximum(jnp.log(1.0 + jnp.exp2(jnp.minimum(z, SOFTPLUS_CLAMP))) * LOG2E, z)


def _suffix_matrix(n):
    return jnp.asarray(np.tril(np.ones((n, n), np.float32), -1), BF16)


def _sb_block(q, kblk, vblk, umat, carry, mask):
    z = lax.dot_general(q, kblk, (((1,), (1,)), ((), ())), preferred_element_type=F32)
    sp = _softplus2(z)
    spm = sp if mask is None else jnp.where(mask, sp, 0.0)
    after = jnp.dot(spm.astype(BF16), umat, preferred_element_type=F32)
    w = jnp.exp2(z - sp - after - carry)
    if mask is not None:
        w = jnp.where(mask, w, 0.0)
    out = jnp.dot(w.astype(BF16), vblk, preferred_element_type=F32)
    return out, carry + after[:, 0:1] + spm[:, 0:1]


def _causal_mask(n):
    r = lax.broadcasted_iota(jnp.int32, (n, n), 0)
    c = lax.broadcasted_iota(jnp.int32, (n, n), 1)
    return c < r


def _sb_prompt_kernel(q_ref, k_ref, v_ref, u_ref, o_ref, acc_ref, carry_ref):
    qi = pl.program_id(2)
    bq = q_ref.shape[0]
    bk = u_ref.shape[0]
    ratio = bq // bk
    q = q_ref[...]
    umat = u_ref[...]

    def sweep(first, n, diagonal):
        s = pl.multiple_of((first - n + 1) * bk, bk)
        keys = k_ref[pl.ds(s, n * bk), :]
        vals = v_ref[pl.ds(s, n * bk), :]
        z = lax.dot_general(q, keys, (((1,), (1,)), ((), ())), preferred_element_type=F32)
        sp = _softplus2(z)
        fail = [sp[:, u * bk:(u + 1) * bk] for u in range(n)]
        if diagonal:
            rows = lax.broadcasted_iota(jnp.int32, (bq, bq), 0)
            cols = lax.broadcasted_iota(jnp.int32, (bq, bq), 1)
            mask = cols < rows
            for t in range(ratio):
                u = n - ratio + t
                fail[u] = jnp.where(mask[:, t * bk:(t + 1) * bk], fail[u], 0.0)
        after = jnp.dot(jnp.concatenate(fail, axis=0).astype(BF16), umat, preferred_element_type=F32)
        carry = jnp.zeros((bq, 1), F32) if diagonal else carry_ref[...]
        offsets = [None] * n
        for u in reversed(range(n)):
            offsets[u] = carry
            carry = carry + after[u * bq:(u + 1) * bq, 0:1] + fail[u][:, 0:1]
        later = jnp.concatenate([after[u * bq:(u + 1) * bq, :] + offsets[u] for u in range(n)], axis=1)
        w = jnp.exp2(z - sp - later)
        if diagonal:
            near = jnp.where(mask, w[:, (n - ratio) * bk:], 0.0)
            w = near if n == ratio else jnp.concatenate([w[:, :(n - ratio) * bk], near], axis=1)
        out = jnp.dot(w.astype(BF16), vals, preferred_element_type=F32)
        if diagonal:
            acc_ref[...] = out
        else:
            acc_ref[...] += out
        carry_ref[...] = carry

    n_blocks = ratio * (qi + 1)
    lead = (n_blocks - 1) % SB_UNROLL + 1
    for n in range(ratio, SB_UNROLL + 1, ratio):
        @pl.when(lead == n)
        def _():
            sweep(n_blocks - 1, n, True)

    def main_body(i, _):
        sweep(n_blocks - 1 - lead - i * SB_UNROLL, SB_UNROLL, False)
        return 0

    lax.fori_loop(0, (n_blocks - lead) // SB_UNROLL, main_body, 0)
    o_ref[...] = acc_ref[...].astype(BF16)


def _sb_prompt(sq, skb, svb, batch, seq):
    bq = _pick_tile(seq, (SB_QUERY_BLOCK, SB_BLOCK, 128, 64))
    bk = min(bq, SB_BLOCK)
    assert SB_UNROLL % (bq // bk) == 0
    nq = seq // bq
    qmap = lambda b, h, i: (b * nq + i, h)
    kvmap = lambda b, h, i: (b, h)
    return pl.pallas_call(
        _sb_prompt_kernel,
        grid=(batch, SB_HEADS, nq),
        in_specs=[pl.BlockSpec((bq, SB_D), qmap),
                  pl.BlockSpec((seq, SB_D), kvmap),
                  pl.BlockSpec((seq, SB_D), kvmap),
                  pl.BlockSpec((bk, bk), lambda b, h, i: (0, 0))],
        out_specs=pl.BlockSpec((bq, SB_D), qmap),
        out_shape=jax.ShapeDtypeStruct((batch * seq, SB_W), BF16),
        scratch_shapes=[pltpu.VMEM((bq, SB_D), F32), pltpu.VMEM((bq, 1), F32)],
        compiler_params=_params("parallel", "parallel", "arbitrary"),
        name="sb_prompt",
    )(sq, skb, svb, _suffix_matrix(bk))


def _sb_sample_kernel(pblk, q_ref, k_ref, v_ref, pk_ref, pv_ref, u_ref, o_ref):
    t = q_ref.shape[0]
    past = pk_ref.shape[0]
    q = q_ref[...]
    acc, carry = _sb_block(q, k_ref[...], v_ref[...], u_ref[:t, :t], jnp.zeros((t, 1), F32), _causal_mask(t))
    for j in reversed(range(past // pblk)):
        kb = pk_ref[j * pblk:(j + 1) * pblk, :].astype(BF16)
        vb = pv_ref[j * pblk:(j + 1) * pblk, :].astype(BF16)
        out, carry = _sb_block(q, kb, vb, u_ref[:pblk, :pblk], carry, None)
        acc = acc + out
    o_ref[...] = acc.astype(BF16)


def _sb_sample(sq, skb, svb, past_k, past_v, batch, seq, past):
    pblk = _pick_tile(past, (SB_BLOCK, 128, 64))
    ublk = max(pblk, seq)
    cur = lambda b, h: (b, h)
    return pl.pallas_call(
        functools.partial(_sb_sample_kernel, pblk),
        grid=(batch, SB_HEADS),
        in_specs=[pl.BlockSpec((seq, SB_D), cur),
                  pl.BlockSpec((seq, SB_D), cur),
                  pl.BlockSpec((seq, SB_D), cur),
                  pl.BlockSpec((past, SB_D), cur),
                  pl.BlockSpec((past, SB_D), cur),
                  pl.BlockSpec((ublk, ublk), lambda b, h: (0, 0))],
        out_specs=pl.BlockSpec((seq, SB_D), cur),
        out_shape=jax.ShapeDtypeStruct((batch * seq, SB_W), BF16),
        compiler_params=_params("parallel", "parallel"),
        name="sb_sample",
    )(sq, skb, svb, past_k, past_v, _suffix_matrix(ublk))


def _merge_kernel(x_ref, r_ref, sb_ref, gt_ref, wr_ref, ws_ref, wo_ref, g2_ref, x1_ref, h2t_ref):
    rb = jnp.dot(r_ref[...], wr_ref[...], preferred_element_type=F32)
    sb = jnp.dot(sb_ref[...], ws_ref[...], preferred_element_type=F32)
    mixed = gt_ref[:, :D_MODEL].astype(F32) * rb + gt_ref[:, D_MODEL:].astype(F32) * sb
    x1 = x_ref[...] + jnp.dot(mixed.astype(BF16), wo_ref[...], preferred_element_type=F32)
    x1_ref[...] = x1
    h2 = x1 * lax.rsqrt(jnp.mean(x1 * x1, axis=-1, keepdims=True) + EPS) * g2_ref[...]
    h2t_ref[...] = h2.T.astype(BF16)


def _merge(x2d, r, sbo, gates, w_ret_o, w_sb_o, w_out, norm2_w):
    n = x2d.shape[0]
    tm = _pick_tile(n, (512, 256, 128))
    row = lambda i: (i, 0)
    const = lambda i: (0, 0)
    return pl.pallas_call(
        _merge_kernel,
        grid=(n // tm,),
        in_specs=[pl.BlockSpec((tm, D_MODEL), row),
                  pl.BlockSpec((tm, RET_V_W), row),
                  pl.BlockSpec((tm, SB_W), row),
                  pl.BlockSpec((tm, 2 * D_MODEL), row),
                  pl.BlockSpec((RET_V_W, D_MODEL), const),
                  pl.BlockSpec((SB_W, D_MODEL), const),
                  pl.BlockSpec((D_MODEL, D_MODEL), const),
                  pl.BlockSpec((1, D_MODEL), const)],
        out_specs=[pl.BlockSpec((tm, D_MODEL), row),
                   pl.BlockSpec((D_MODEL, tm), lambda i: (0, i))],
        out_shape=[jax.ShapeDtypeStruct((n, D_MODEL), F32),
                   jax.ShapeDtypeStruct((D_MODEL, n), BF16)],
        compiler_params=_params("parallel"),
        name="merge",
    )(x2d, r, sbo, gates, w_ret_o, w_sb_o, w_out, norm2_w.reshape(1, D_MODEL))


PEER_SLAB = 1024


def _sort_desc(a):
    a = list(a)
    n = len(a)
    k = 2
    while k <= n:
        j = k // 2
        while j >= 1:
            for i in range(n):
                l = i ^ j
                if l > i:
                    hi = jnp.maximum(a[i], a[l])
                    lo = jnp.minimum(a[i], a[l])
                    a[i], a[l] = (hi, lo) if (i & k) == 0 else (lo, hi)
            j //= 2
        k *= 2
    return a


def _merge_bitonic_desc(a):
    a = list(a)
    n = len(a)
    j = n // 2
    while j >= 1:
        for i in range(n):
            l = i ^ j
            if l > i:
                a[i], a[l] = jnp.maximum(a[i], a[l]), jnp.minimum(a[i], a[l])
        j //= 2
    return a


def _top_merge(a, b):
    n = len(a)
    return _merge_bitonic_desc([jnp.maximum(a[i], b[n - 1 - i]) for i in range(n)])


def _top16_rows(s):
    groups = s.shape[0] // 8
    a = _sort_desc([s[8 * j:8 * j + 8, :] for j in range(groups)])
    for shift in (4, 2, 1):
        a = _top_merge(a, [pltpu.roll(x, shift, 0) for x in a])
    return a


def _gelu_tanh(x):
    c0 = math.sqrt(2.0 / math.pi)
    half = 0.5 * x
    return half + half * jnp.tanh(x * (c0 + (c0 * 0.044715) * (x * x)))


def _twin_bf16_words(x):
    hi = pltpu.bitcast(x.astype(BF16).astype(F32), jnp.uint32)
    return hi | (hi >> 16)


def _lookup_desc(keys, vals, default, s):
    rows, tn = s.shape
    s3 = s.reshape(rows // 8, 8, tn)
    out = jnp.full_like(s3, default)
    for b in reversed(range(len(keys))):
        v = vals[b] if isinstance(vals[b], float) else vals[b][None]
        out = jnp.where(s3 == keys[b][None], v, out)
    return out.reshape(rows, tn)


def _peer_kernel(n_slabs, h2t_ref, x1_ref, wq_ref, keys_ref, u_ref, vt_ref, gf_ref, y_ref,
                 s1_ref, s2_ref, cnt_ref, e1_ref, rank2_ref, e2_ref, coef0_ref, coef1_ref, gate0_ref, gate1_ref,
                 acc_ref):
    j = pl.program_id(1)
    ns = n_slabs
    coef_refs = (coef0_ref, coef1_ref)
    gate_refs = (gate0_ref, gate1_ref)
    tn = h2t_ref.shape[1]
    ht = h2t_ref[...]

    @pl.when(j == 0)
    def _():
        qt = jnp.dot(wq_ref[...], ht, preferred_element_type=F32).astype(BF16)
        for p, dst in ((0, s1_ref), (1, s2_ref)):
            for hh in range(PEER_HEADS):
                r0 = (p * PEER_HEADS + hh) * PEER_HALF
                dst[hh] = jnp.dot(keys_ref[p, hh], qt[r0:r0 + PEER_HALF, :], preferred_element_type=F32)

        lw = 128

        def selection_tables(ci, _):
            lanes = pl.ds(pl.multiple_of(ci * lw, lw), lw)
            sub = lax.broadcasted_iota(jnp.int32, (8, lw), 0)
            v1 = [jnp.zeros((8, lw), F32) for _ in range(PEER_TOPK)]
            v2 = [jnp.zeros((8, lw), F32) for _ in range(PEER_TOPK)]
            ranks = [float(b) for b in range(PEER_TOPK)]
            for hh in range(PEER_HEADS):
                top = _top16_rows(s1_ref[hh, :, lanes])
                v1 = [jnp.where(sub == hh, top[a], v1[a]) for a in range(PEER_TOPK)]
                s2 = s2_ref[hh, :, lanes]
                top = _top16_rows(s2)
                v2 = [jnp.where(sub == hh, top[a], v2[a]) for a in range(PEER_TOPK)]
                rank2_ref[hh, :, lanes] = _lookup_desc(top, ranks, float(PEER_TOPK), s2).astype(BF16)
                e2 = jnp.exp(s2.reshape(N_KEYS // 8, 8, lw) - top[0][None])
                e2_ref[hh, :, lanes] = e2.reshape(N_KEYS, lw).astype(BF16)
            cand_rows = [[v1[a] + v2[b] for b in range(PEER_TOPK // (a + 1))] for a in range(PEER_TOPK)]
            cands = [c for row in cand_rows for c in row]
            pad = [jnp.full((8, lw), NEG_BIG, F32)] * (-len(cands) % PEER_TOPK)
            padded = cands + pad
            lists = [_sort_desc(padded[i:i + PEER_TOPK]) for i in range(0, len(padded), PEER_TOPK)]
            while len(lists) > 1:
                lists = [_top_merge(lists[i], lists[i + 1]) for i in range(0, len(lists), 2)]
            tau = lists[0][PEER_TOPK - 1]
            m = cands[0]
            z = jnp.zeros((8, lw), F32)
            for cnd in cands:
                z = z + jnp.where(cnd >= tau, jnp.exp(cnd - m), 0.0)
            shift = v1[0] + jnp.log(z)
            cnt_rows = []
            for row in cand_rows:
                n_sel = jnp.zeros((8, lw), F32)
                for cnd in row:
                    n_sel = n_sel + jnp.where(cnd >= tau, 1.0, 0.0)
                cnt_rows.append(n_sel)
            for hh in range(PEER_HEADS):
                s1 = s1_ref[hh, :, lanes]
                head_row = lambda x: jnp.broadcast_to(x[hh:hh + 1, :], (8, lw))
                cnt = _lookup_desc([head_row(v) for v in v1], [head_row(c) for c in cnt_rows], 0.0, s1)
                cnt_ref[hh, :, lanes] = _twin_bf16_words(cnt)
                e1_ref[hh, :, lanes] = _twin_bf16_words(jnp.exp(s1 - shift[hh:hh + 1, :]))
            return 0

        lax.fori_loop(0, tn // lw, selection_tables, 0)
        acc_ref[...] = jnp.zeros_like(acc_ref)
        coef_refs[1][...] = jnp.zeros_like(coef_refs[1])

    n_i1 = u_ref.shape[0] // N_KEYS
    pack = 16

    def gate_rows(slab):
        out = []
        for ii in range(n_i1):
            i1 = jnp.minimum(slab * n_i1 + ii, N_KEYS - 1)
            out.append([(cnt_ref[hh, pl.ds(i1, 1), :], e1_ref[hh, pl.ds(i1, 1), :])
                        for hh in range(PEER_HEADS)])
        return out

    def gate_group(dst_ref, table_rows, ii):
        as_tile = lambda row: pltpu.bitcast(jnp.broadcast_to(row, (pack // 2, tn)), BF16)
        cnt_b = [as_tile(c) for c, _ in table_rows[ii]]
        e1_b = [as_tile(e) for _, e in table_rows[ii]]
        for c in range(N_KEYS // pack):
            rows = slice(c * pack, (c + 1) * pack)
            g = jnp.zeros((pack, tn), BF16)
            for hh in range(PEER_HEADS):
                w = e2_ref[hh, rows, :] * e1_b[hh]
                g = g + jnp.where(rank2_ref[hh, rows, :] < cnt_b[hh], w, jnp.zeros_like(w))
            r0 = ii * N_KEYS + c * pack
            dst_ref[r0:r0 + pack, :] = g

    @pl.when(j == 0)
    def _():
        rows0 = gate_rows(0)
        for ii in range(n_i1):
            gate_group(gate_refs[0], rows0, ii)

    def add_previous_slab(prev_ref):
        acc_ref[...] += jnp.dot(vt_ref[...], prev_ref[...], preferred_element_type=F32)

    def build_slab(parity):
        cur_ref, prev_ref = coef_refs[parity], coef_refs[1 - parity]
        gate_ref, next_gate_ref = gate_refs[parity], gate_refs[1 - parity]
        next_rows = gate_rows(j + 1)
        n_split = 2 if tn % 512 == 0 else 1
        wn = tn // n_split
        m_split = 2
        em = u_ref.shape[0] // m_split
        dm = D_MODEL // m_split
        gm = em // N_KEYS
        assert n_i1 == m_split * gm

        dots = {}

        def expert_dot_quadrant(mi, ni):
            dots[mi, ni] = jnp.dot(u_ref[mi * em:(mi + 1) * em, :], h2t_ref[:, ni * wn:(ni + 1) * wn],
                                   preferred_element_type=F32)

        def previous_slab_quadrant(mi, ni):
            blk = (slice(mi * dm, (mi + 1) * dm), slice(ni * wn, (ni + 1) * wn))
            acc_ref[blk] += jnp.dot(vt_ref[mi * dm:(mi + 1) * dm, :], prev_ref[:, ni * wn:(ni + 1) * wn],
                                    preferred_element_type=F32)

        def coef_group(ii):
            mi, local = divmod(ii, gm)
            for c in range(N_KEYS // pack):
                for ni in range(n_split):
                    lr = local * N_KEYS + c * pack
                    act = _gelu_tanh(dots[mi, ni][lr:lr + pack, :].astype(BF16))
                    r0 = ii * N_KEYS + c * pack
                    cols = slice(ni * wn, (ni + 1) * wn)
                    cur_ref[r0:r0 + pack, cols] = gate_ref[r0:r0 + pack, cols] * act

        quads = [(mi, ni) for mi in range(m_split) for ni in range(n_split)]
        matmuls = ([functools.partial(expert_dot_quadrant, *q) for q in quads]
                   + [functools.partial(previous_slab_quadrant, *q) for q in quads])
        tasks = [[] for _ in matmuls]
        first_coef_slot = n_split
        for ii in range(n_i1):
            slot = min(first_coef_slot + ii // 2, len(tasks) - 1)
            slot = max(slot, (ii // gm + 1) * n_split)
            tasks[min(slot, len(tasks) - 1)].append(functools.partial(coef_group, ii))
        for ii in range(n_i1):
            tasks[ii * len(tasks) // n_i1].append(functools.partial(gate_group, next_gate_ref, next_rows, ii))
        for mm, work in zip(matmuls, tasks):
            mm()
            for w in work:
                w()

    for parity in range(2):
        @pl.when(jnp.logical_and(j < ns, j % 2 == parity))
        def _():
            build_slab(parity)

    @pl.when(j == ns)
    def _():
        add_previous_slab(coef_refs[(n_slabs - 1) % 2])
        x2 = x1_ref[...] + acc_ref[...].T
        y = x2 * lax.rsqrt(jnp.mean(x2 * x2, axis=-1, keepdims=True) + EPS) * gf_ref[...]
        y_ref[...] = y


def _peer(h2t, x1, wq_t, keys_bf, u_bf, vt_bf, norm_f_w):
    n = x1.shape[0]
    tn = _pick_tile(n, (512, 256, 128))
    ns = N_EXPERTS // PEER_SLAB
    table = (PEER_HEADS, N_KEYS, tn)
    return pl.pallas_call(
        functools.partial(_peer_kernel, ns),
        grid=(n // tn, ns + 1),
        in_specs=[pl.BlockSpec((D_MODEL, tn), lambda i, j: (0, i)),
                  pl.BlockSpec((tn, D_MODEL), lambda i, j: (i, 0)),
                  pl.BlockSpec((2 * PEER_HEADS * PEER_HALF, D_MODEL), lambda i, j: (0, 0)),
                  pl.BlockSpec((2, PEER_HEADS, N_KEYS, PEER_HALF), lambda i, j: (0, 0, 0, 0)),
                  pl.BlockSpec((PEER_SLAB, D_MODEL), lambda i, j: (jnp.minimum(j, ns - 1), 0)),
                  pl.BlockSpec((D_MODEL, PEER_SLAB), lambda i, j: (0, jnp.maximum(j - 1, 0))),
                  pl.BlockSpec((1, D_MODEL), lambda i, j: (0, 0))],
        out_specs=pl.BlockSpec((tn, D_MODEL), lambda i, j: (i, 0)),
        out_shape=jax.ShapeDtypeStruct((n, D_MODEL), F32),
        scratch_shapes=[pltpu.VMEM(table, F32),
                        pltpu.VMEM(table, F32),
                        pltpu.VMEM(table, jnp.uint32),
                        pltpu.VMEM(table, jnp.uint32),
                        pltpu.VMEM(table, BF16),
                        pltpu.VMEM(table, BF16),
                        pltpu.VMEM((PEER_SLAB, tn), BF16),
                        pltpu.VMEM((PEER_SLAB, tn), BF16),
                        pltpu.VMEM((PEER_SLAB, tn), BF16),
                        pltpu.VMEM((PEER_SLAB, tn), BF16),
                        pltpu.VMEM((D_MODEL, tn), F32)],
        compiler_params=_params("parallel", "arbitrary"),
        name="peer",
    )(h2t, x1, wq_t, keys_bf, u_bf, vt_bf, norm_f_w.reshape(1, D_MODEL))


def _layer(x, pos, past_k, past_v, s0, wts):
    batch, seq, _ = x.shape
    n = batch * seq
    x2d = x.reshape(n, D_MODEL)
    cos_tab, sin_tab = _rotary_tables(pos)
    tm = _pick_tile(n, (512, 256, 128, 64))
    if seq < tm:
        cos_tab = jnp.tile(cos_tab, (tm // seq, 1))
        sin_tab = jnp.tile(sin_tab, (tm // seq, 1))
    rq, rk, rv, gate, sq, sk, sv, skb, svb, gates = _in_proj(x2d, wts["norm1_w"], wts["w_in"], cos_tab, sin_tab)
    r, s_fin = _retention(rq, rk, rv, gate, s0, wts["ret_gn_w"], batch, seq)
    if past_k is None:
        sbo = _sb_prompt(sq, skb, svb, batch, seq)
    else:
        sbo = _sb_sample(sq, skb, svb, past_k, past_v, batch, seq, past_k.shape[0] // batch)
    x1, h2t = _merge(x2d, r, sbo, gates, wts["w_ret_o"], wts["w_sb_o"], wts["w_out"], wts["norm2_w"])
    y = _peer(h2t, x1, wts["w_pq_t"], wts["pk_keys"], wts["peer_u"], wts["peer_v_t"], wts["norm_f_w"])
    return (y.reshape(batch, seq, D_MODEL), sk.reshape(batch, seq, SB_HEADS, SB_D),
            sv.reshape(batch, seq, SB_HEADS, SB_D), s_fin)


def kernel(x_prompt, x_sample, cache_sb_k, cache_sb_v, state_ret, norm1_w, w_in, ret_gn_w, w_ret_o, w_sb_o,
           w_out, norm2_w, w_pq, pk_keys, peer_u, peer_v, norm_f_w):
    assert norm1_w.shape[0] == 1, "single-layer model"
    wq_t = w_pq[0].reshape(D_MODEL, PEER_HEADS, 2, PEER_HALF).transpose(2, 1, 3, 0)
    wq_t = wq_t.reshape(2 * PEER_HEADS * PEER_HALF, D_MODEL).astype(BF16)
    wts = dict(
        norm1_w=norm1_w[0], w_in=w_in[0].astype(BF16), ret_gn_w=ret_gn_w[0],
        w_ret_o=w_ret_o[0].astype(BF16), w_sb_o=w_sb_o[0].astype(BF16), w_out=w_out[0].astype(BF16),
        norm2_w=norm2_w[0], w_pq_t=wq_t, pk_keys=pk_keys[0].astype(BF16),
        peer_u=peer_u[0].astype(BF16), peer_v_t=peer_v[0].T.astype(BF16), norm_f_w=norm_f_w)
    bp, tp, _ = x_prompt.shape
    bs, ts, _ = x_sample.shape
    past = cache_sb_k.shape[2]
    yp, kp, vp, sp = _layer(x_prompt, jnp.arange(tp, dtype=jnp.int32), None, None,
                            jnp.zeros((bp, RET_HEADS, RET_DK, RET_DV), F32), wts)
    ys, ks, vs, ss = _layer(x_sample, past + jnp.arange(ts, dtype=jnp.int32),
                            cache_sb_k[0].reshape(bs * past, SB_W), cache_sb_v[0].reshape(bs * past, SB_W),
                            state_ret[0], wts)
    return (yp, ys, kp[None], vp[None], sp[None], ks[None], vs[None], ss[None])
```

```python
import functools
import math

import numpy as np
import jax
import jax.numpy as jnp
from jax import lax
from jax.experimental import pallas as pl
from jax.experimental.pallas import tpu as pltpu

F32 = jnp.float32
BF16 = jnp.bfloat16

D_MODEL = 1024
CHUNK = 64
RET_HEADS = 4
RET_DK = 128
RET_DV = 256
SB_HEADS = 4
SB_D = 128
N_KEYS = 128
N_EXPERTS = N_KEYS * N_KEYS
PEER_HEADS = 8
PEER_HALF = 128
PEER_TOPK = 16
ROPE_BASE = 10000.0
EPS = 1e-6
RET_QK_W = RET_HEADS * RET_DK
RET_V_W = RET_HEADS * RET_DV
SB_W = SB_HEADS * SB_D
IN_W = 2 * RET_QK_W + 2 * RET_V_W + 3 * SB_W + 2 * D_MODEL

VMEM_LIMIT_BYTES = 56 * 1024 * 1024
NEG_BIG = -1e30


def _pick_tile(n, candidates):
    for c in candidates:
        if n % c == 0:
            return c
    raise ValueError(f"no tile in {candidates} divides {n}")


def _params(*sem):
    return pltpu.CompilerParams(dimension_semantics=sem, vmem_limit_bytes=VMEM_LIMIT_BYTES)


def _in_proj_kernel(x_ref, g_ref, w_ref, cos_ref, sin_ref,
                    rq_ref, rk_ref, rv_ref, rg_ref, sq_ref, sk_ref, sv_ref, skb_ref, svb_ref, gt_ref):
    x = x_ref[...]
    h = (x * lax.rsqrt(jnp.mean(x * x, axis=-1, keepdims=True) + EPS) * g_ref[...]).astype(BF16)

    def proj(lo, width):
        return jnp.dot(h, w_ref[:, lo:lo + width], preferred_element_type=F32)

    c = cos_ref[...]
    s = sin_ref[...]

    def rotary_store(p, out_ref, scale):
        for hh in range(RET_HEADS):
            ph = p[:, hh * RET_DK:(hh + 1) * RET_DK]
            r = ph * c + pltpu.roll(ph, RET_DK // 2, 1) * s
            if scale != 1.0:
                r = r * scale
            out_ref[:, hh * RET_DK:(hh + 1) * RET_DK] = r.astype(BF16)

    rotary_store(proj(0, RET_QK_W), rq_ref, 1.0)
    rotary_store(proj(RET_QK_W, RET_QK_W), rk_ref, RET_DK ** -0.5)
    off = 2 * RET_QK_W
    for j in range(RET_V_W // 512):
        rv_ref[:, j * 512:(j + 1) * 512] = proj(off + j * 512, 512).astype(BF16)
    off += RET_V_W
    for j in range(RET_V_W // 512):
        g = proj(off + j * 512, 512)
        rg_ref[:, j * 512:(j + 1) * 512] = (g / (1.0 + jnp.exp(-g))).astype(BF16)
    off += RET_V_W
    sq_ref[...] = (proj(off, SB_W) * (SB_D ** -0.5 * LOG2E)).astype(BF16)
    off += SB_W
    tm = x_ref.shape[0]

    def store_rows_by_head(out_ref, val):
        for hh in range(SB_HEADS):
            out_ref[pl.ds(hh, tm, stride=SB_HEADS), :] = val[:, hh * SB_D:(hh + 1) * SB_D]

    k = proj(off, SB_W)
    store_rows_by_head(sk_ref, k)
    skb_ref[...] = k.astype(BF16)
    off += SB_W
    v = proj(off, SB_W)
    store_rows_by_head(sv_ref, v)
    svb_ref[...] = v.astype(BF16)
    off += SB_W
    for j in range(2 * D_MODEL // 512):
        g = proj(off + j * 512, 512)
        gt_ref[:, j * 512:(j + 1) * 512] = (1.0 / (1.0 + jnp.exp(-g))).astype(BF16)


def _in_proj(x2d, norm_w, w_in_bf, cos_tab, sin_tab):
    n = x2d.shape[0]
    tm = _pick_tile(n, (512, 256, 128, 64))
    assert cos_tab.shape[0] % tm == 0
    npos = cos_tab.shape[0] // tm
    row = lambda i: (i, 0)
    const = lambda i: (0, 0)
    pos = lambda i: (i % npos, 0)
    outs = [(1, RET_QK_W, BF16), (1, RET_QK_W, BF16), (1, RET_V_W, BF16), (1, RET_V_W, BF16), (1, SB_W, BF16),
            (SB_HEADS, SB_D, F32), (SB_HEADS, SB_D, F32), (1, SB_W, BF16), (1, SB_W, BF16),
            (1, 2 * D_MODEL, BF16)]
    return pl.pallas_call(
        _in_proj_kernel,
        grid=(n // tm,),
        in_specs=[pl.BlockSpec((tm, D_MODEL), row),
                  pl.BlockSpec((1, D_MODEL), const),
                  pl.BlockSpec((D_MODEL, IN_W), const),
                  pl.BlockSpec((tm, RET_DK), pos),
                  pl.BlockSpec((tm, RET_DK), pos)],
        out_specs=[pl.BlockSpec((tm * r, w), row) for r, w, _ in outs],
        out_shape=[jax.ShapeDtypeStruct((n * r, w), dt) for r, w, dt in outs],
        compiler_params=_params("parallel"),
        name="in_proj",
    )(x2d, norm_w.reshape(1, D_MODEL), w_in_bf, cos_tab, sin_tab)


def _rotary_tables(pos):
    inv = ROPE_BASE ** (-jnp.arange(0, RET_DK, 2, dtype=F32) / RET_DK)
    ang = pos.astype(F32)[:, None] * inv[None, :]
    cos = jnp.cos(ang)
    sin = jnp.sin(ang)
    return jnp.concatenate([cos, cos], axis=1), jnp.concatenate([-sin, sin], axis=1)


def _retention_tables(blk):
    log_g = np.log(1.0 - 2.0 ** (-5.0 - np.arange(RET_HEADS, dtype=np.float64)))
    i = np.arange(blk, dtype=np.float64)
    dist = np.abs(i[:, None] - i[None, :])
    visible = (np.arange(blk)[None, :] // CHUNK) <= (np.arange(blk)[:, None] // CHUNK)
    dmat = np.where(visible[None], np.exp(dist[None] * log_g[:, None, None]), 0.0)
    inter = np.exp((i[None, :] + 1.0) * log_g[:, None])
    kdec = np.exp((blk - 1.0 - i)[None, :] * log_g[:, None])
    inter = np.broadcast_to(inter[:, :, None], (RET_HEADS, blk, RET_DV))
    kdec = np.broadcast_to(kdec[:, :, None], (RET_HEADS, blk, RET_DK))
    sdec = tuple(float(np.exp(blk * g)) for g in log_g)
    return (jnp.asarray(dmat, F32), jnp.asarray(inter, F32), jnp.asarray(kdec, F32)), sdec


def _retention_kernel(sdec, q_ref, k_ref, v_ref, gate_ref, s0_ref, dmat_ref, inter_ref, kdec_ref, gn_ref,
                      r_ref, sfin_ref, state_ref):
    t = pl.program_id(1)

    @pl.when(t == 0)
    def _():
        state_ref[...] = s0_ref[0]

    for hh in range(RET_HEADS):
        q = q_ref[:, hh * RET_DK:(hh + 1) * RET_DK]
        k = k_ref[:, hh * RET_DK:(hh + 1) * RET_DK]
        v = v_ref[:, hh * RET_DV:(hh + 1) * RET_DV]
        state = state_ref[hh]
        sc = lax.dot_general(q, k, (((1,), (1,)), ((), ())), preferred_element_type=F32)
        p = (sc * dmat_ref[hh]).astype(BF16)
        o = jnp.dot(p, v, preferred_element_type=F32)
        o = o + jnp.dot(q, state.astype(BF16), preferred_element_type=F32) * inter_ref[hh]
        kd = (k.astype(F32) * kdec_ref[hh]).astype(BF16)
        upd = lax.dot_general(kd, v, (((0,), (0,)), ((), ())), preferred_element_type=F32)
        state_ref[hh] = sdec[hh] * state + upd
        mu = jnp.mean(o, axis=-1, keepdims=True)
        cen = o - mu
        var = jnp.mean(cen * cen, axis=-1, keepdims=True)
        r = cen * lax.rsqrt(var + EPS) * gn_ref[:, hh * RET_DV:(hh + 1) * RET_DV]
        r = r * gate_ref[:, hh * RET_DV:(hh + 1) * RET_DV].astype(F32)
        r_ref[:, hh * RET_DV:(hh + 1) * RET_DV] = r.astype(BF16)

    @pl.when(t == pl.num_programs(1) - 1)
    def _():
        sfin_ref[0] = state_ref[...]


def _retention(rq, rk, rv, gate, s0, gn_w, batch, seq):
    blk = _pick_tile(seq, (256, 128, 64))
    nt = seq // blk
    tabs, sdec = _retention_tables(blk)
    row = lambda b, t: (b * nt + t, 0)
    per_b = lambda b, t: (b, 0, 0, 0)
    const3 = lambda b, t: (0, 0, 0)
    return pl.pallas_call(
        functools.partial(_retention_kernel, sdec),
        grid=(batch, nt),
        in_specs=[pl.BlockSpec((blk, RET_QK_W), row),
                  pl.BlockSpec((blk, RET_QK_W), row),
                  pl.BlockSpec((blk, RET_V_W), row),
                  pl.BlockSpec((blk, RET_V_W), row),
                  pl.BlockSpec((1, RET_HEADS, RET_DK, RET_DV), per_b),
                  pl.BlockSpec((RET_HEADS, blk, blk), const3),
                  pl.BlockSpec((RET_HEADS, blk, RET_DV), const3),
                  pl.BlockSpec((RET_HEADS, blk, RET_DK), const3),
                  pl.BlockSpec((1, RET_V_W), lambda b, t: (0, 0))],
        out_specs=[pl.BlockSpec((blk, RET_V_W), row),
                   pl.BlockSpec((1, RET_HEADS, RET_DK, RET_DV), per_b)],
        out_shape=[jax.ShapeDtypeStruct((batch * seq, RET_V_W), BF16),
                   jax.ShapeDtypeStruct((batch, RET_HEADS, RET_DK, RET_DV), F32)],
        scratch_shapes=[pltpu.VMEM((RET_HEADS, RET_DK, RET_DV), F32)],
        compiler_params=_params("parallel", "arbitrary"),
        name="retention",
    )(rq, rk, rv, gate, s0, *tabs, gn_w.reshape(1, RET_V_W))


SB_BLOCK = 256
SB_QUERY_BLOCK = 512
SB_UNROLL = 8
LOG2E = math.log2(math.e)
SOFTPLUS_CLAMP = 30.0 * LOG2E


def _softplus2(z):
    return jnp.maximum(jnp.log(1.0 + jnp.exp2(jnp.minimum(z, SOFTPLUS_CLAMP))) * LOG2E, z)


def _suffix_matrix(n):
    return jnp.asarray(np.tril(np.ones((n, n), np.float32), -1), BF16)


def _sb_block(q, kblk, vblk, umat, carry, mask):
    z = lax.dot_general(q, kblk, (((1,), (1,)), ((), ())), preferred_element_type=F32)
    sp = _softplus2(z)
    spm = sp if mask is None else jnp.where(mask, sp, 0.0)
    after = jnp.dot(spm.astype(BF16), umat, preferred_element_type=F32)
    w = jnp.exp2(z - sp - after - carry)
    if mask is not None:
        w = jnp.where(mask, w, 0.0)
    out = jnp.dot(w.astype(BF16), vblk, preferred_element_type=F32)
    return out, carry + after[:, 0:1] + spm[:, 0:1]


def _causal_mask(n):
    r = lax.broadcasted_iota(jnp.int32, (n, n), 0)
    c = lax.broadcasted_iota(jnp.int32, (n, n), 1)
    return c < r


def _sb_prompt_kernel(q_ref, k_ref, v_ref, u_ref, o_ref, acc_ref, carry_ref):
    qi = pl.program_id(2)
    bq = q_ref.shape[0]
    bk = u_ref.shape[0]
    ratio = bq // bk
    q = q_ref[...]
    umat = u_ref[...]

    def sweep(first, n, diagonal):
        s = pl.multiple_of((first - n + 1) * bk, bk)
        keys = k_ref[pl.ds(s, n * bk), :]
        vals = v_ref[pl.ds(s, n * bk), :]
        z = lax.dot_general(q, keys, (((1,), (1,)), ((), ())), preferred_element_type=F32)
        sp = _softplus2(z)
        fail = [sp[:, u * bk:(u + 1) * bk] for u in range(n)]
        if diagonal:
            rows = lax.broadcasted_iota(jnp.int32, (bq, bq), 0)
            cols = lax.broadcasted_iota(jnp.int32, (bq, bq), 1)
            mask = cols < rows
            for t in range(ratio):
                u = n - ratio + t
                fail[u] = jnp.where(mask[:, t * bk:(t + 1) * bk], fail[u], 0.0)
        after = jnp.dot(jnp.concatenate(fail, axis=0).astype(BF16), umat, preferred_element_type=F32)
        carry = jnp.zeros((bq, 1), F32) if diagonal else carry_ref[...]
        offsets = [None] * n
        for u in reversed(range(n)):
            offsets[u] = carry
            carry = carry + after[u * bq:(u + 1) * bq, 0:1] + fail[u][:, 0:1]
        later = jnp.concatenate([after[u * bq:(u + 1) * bq, :] + offsets[u] for u in range(n)], axis=1)
        w = jnp.exp2(z - sp - later)
        if diagonal:
            near = jnp.where(mask, w[:, (n - ratio) * bk:], 0.0)
            w = near if n == ratio else jnp.concatenate([w[:, :(n - ratio) * bk], near], axis=1)
        out = jnp.dot(w.astype(BF16), vals, preferred_element_type=F32)
        if diagonal:
            acc_ref[...] = out
        else:
            acc_ref[...] += out
        carry_ref[...] = carry

    n_blocks = ratio * (qi + 1)
    lead = (n_blocks - 1) % SB_UNROLL + 1
    for n in range(ratio, SB_UNROLL + 1, ratio):
        @pl.when(lead == n)
        def _():
            sweep(n_blocks - 1, n, True)

    def main_body(i, _):
        sweep(n_blocks - 1 - lead - i * SB_UNROLL, SB_UNROLL, False)
        return 0

    lax.fori_loop(0, (n_blocks - lead) // SB_UNROLL, main_body, 0)
    o_ref[...] = acc_ref[...].astype(BF16)


def _sb_prompt(sq, skb, svb, batch, seq):
    bq = _pick_tile(seq, (SB_QUERY_BLOCK, SB_BLOCK, 128, 64))
    bk = min(bq, SB_BLOCK)
    assert SB_UNROLL % (bq // bk) == 0
    nq = seq // bq
    qmap = lambda b, h, i: (b * nq + i, h)
    kvmap = lambda b, h, i: (b, h)
    return pl.pallas_call(
        _sb_prompt_kernel,
        grid=(batch, SB_HEADS, nq),
        in_specs=[pl.BlockSpec((bq, SB_D), qmap),
                  pl.BlockSpec((seq, SB_D), kvmap),
                  pl.BlockSpec((seq, SB_D), kvmap),
                  pl.BlockSpec((bk, bk), lambda b, h, i: (0, 0))],
        out_specs=pl.BlockSpec((bq, SB_D), qmap),
        out_shape=jax.ShapeDtypeStruct((batch * seq, SB_W), BF16),
        scratch_shapes=[pltpu.VMEM((bq, SB_D), F32), pltpu.VMEM((bq, 1), F32)],
        compiler_params=_params("parallel", "parallel", "arbitrary"),
        name="sb_prompt",
    )(sq, skb, svb, _suffix_matrix(bk))


def _sb_sample_kernel(pblk, q_ref, k_ref, v_ref, pk_ref, pv_ref, u_ref, o_ref):
    t = q_ref.shape[0]
    past = pk_ref.shape[0]
    q = q_ref[...]
    acc, carry = _sb_block(q, k_ref[...], v_ref[...], u_ref[:t, :t], jnp.zeros((t, 1), F32), _causal_mask(t))
    for j in reversed(range(past // pblk)):
        kb = pk_ref[j * pblk:(j + 1) * pblk, :].astype(BF16)
        vb = pv_ref[j * pblk:(j + 1) * pblk, :].astype(BF16)
        out, carry = _sb_block(q, kb, vb, u_ref[:pblk, :pblk], carry, None)
        acc = acc + out
    o_ref[...] = acc.astype(BF16)


def _sb_sample(sq, skb, svb, past_k, past_v, batch, seq, past):
    pblk = _pick_tile(past, (SB_BLOCK, 128, 64))
    ublk = max(pblk, seq)
    cur = lambda b, h: (b, h)
    return pl.pallas_call(
        functools.partial(_sb_sample_kernel, pblk),
        grid=(batch, SB_HEADS),
        in_specs=[pl.BlockSpec((seq, SB_D), cur),
                  pl.BlockSpec((seq, SB_D), cur),
                  pl.BlockSpec((seq, SB_D), cur),
                  pl.BlockSpec((past, SB_D), cur),
                  pl.BlockSpec((past, SB_D), cur),
                  pl.BlockSpec((ublk, ublk), lambda b, h: (0, 0))],
        out_specs=pl.BlockSpec((seq, SB_D), cur),
        out_shape=jax.ShapeDtypeStruct((batch * seq, SB_W), BF16),
        compiler_params=_params("parallel", "parallel"),
        name="sb_sample",
    )(sq, skb, svb, past_k, past_v, _suffix_matrix(ublk))


def _merge_kernel(x_ref, r_ref, sb_ref, gt_ref, wr_ref, ws_ref, wo_ref, g2_ref, x1_ref, h2t_ref):
    rb = jnp.dot(r_ref[...], wr_ref[...], preferred_element_type=F32)
    sb = jnp.dot(sb_ref[...], ws_ref[...], preferred_element_type=F32)
    mixed = gt_ref[:, :D_MODEL].astype(F32) * rb + gt_ref[:, D_MODEL:].astype(F32) * sb
    x1 = x_ref[...] + jnp.dot(mixed.astype(BF16), wo_ref[...], preferred_element_type=F32)
    x1_ref[...] = x1
    h2 = x1 * lax.rsqrt(jnp.mean(x1 * x1, axis=-1, keepdims=True) + EPS) * g2_ref[...]
    h2t_ref[...] = h2.T.astype(BF16)


def _merge(x2d, r, sbo, gates, w_ret_o, w_sb_o, w_out, norm2_w):
    n = x2d.shape[0]
    tm = _pick_tile(n, (512, 256, 128))
    row = lambda i: (i, 0)
    const = lambda i: (0, 0)
    return pl.pallas_call(
        _merge_kernel,
        grid=(n // tm,),
        in_specs=[pl.BlockSpec((tm, D_MODEL), row),
                  pl.BlockSpec((tm, RET_V_W), row),
                  pl.BlockSpec((tm, SB_W), row),
                  pl.BlockSpec((tm, 2 * D_MODEL), row),
                  pl.BlockSpec((RET_V_W, D_MODEL), const),
                  pl.BlockSpec((SB_W, D_MODEL), const),
                  pl.BlockSpec((D_MODEL, D_MODEL), const),
                  pl.BlockSpec((1, D_MODEL), const)],
        out_specs=[pl.BlockSpec((tm, D_MODEL), row),
                   pl.BlockSpec((D_MODEL, tm), lambda i: (0, i))],
        out_shape=[jax.ShapeDtypeStruct((n, D_MODEL), F32),
                   jax.ShapeDtypeStruct((D_MODEL, n), BF16)],
        compiler_params=_params("parallel"),
        name="merge",
    )(x2d, r, sbo, gates, w_ret_o, w_sb_o, w_out, norm2_w.reshape(1, D_MODEL))


PEER_SLAB = 1024


def _sort_desc(a):
    a = list(a)
    n = len(a)
    k = 2
    while k <= n:
        j = k // 2
        while j >= 1:
            for i in range(n):
                l = i ^ j
                if l > i:
                    hi = jnp.maximum(a[i], a[l])
                    lo = jnp.minimum(a[i], a[l])
                    a[i], a[l] = (hi, lo) if (i & k) == 0 else (lo, hi)
            j //= 2
        k *= 2
    return a


def _merge_bitonic_desc(a):
    a = list(a)
    n = len(a)
    j = n // 2
    while j >= 1:
        for i in range(n):
            l = i ^ j
            if l > i:
                a[i], a[l] = jnp.maximum(a[i], a[l]), jnp.minimum(a[i], a[l])
        j //= 2
    return a


def _top_merge(a, b):
    n = len(a)
    return _merge_bitonic_desc([jnp.maximum(a[i], b[n - 1 - i]) for i in range(n)])


def _top16_rows(s):
    groups = s.shape[0] // 8
    a = _sort_desc([s[8 * j:8 * j + 8, :] for j in range(groups)])
    for shift in (4, 2, 1):
        a = _top_merge(a, [pltpu.roll(x, shift, 0) for x in a])
    return a


def _gelu_tanh(x):
    c0 = math.sqrt(2.0 / math.pi)
    half = 0.5 * x
    return half + half * jnp.tanh(x * (c0 + (c0 * 0.044715) * (x * x)))


def _twin_bf16_words(x):
    hi = pltpu.bitcast(x.astype(BF16).astype(F32), jnp.uint32)
    return hi | (hi >> 16)


def _lookup_desc(keys, vals, default, s):
    rows, tn = s.shape
    s3 = s.reshape(rows // 8, 8, tn)
    out = jnp.full_like(s3, default)
    for b in reversed(range(len(keys))):
        v = vals[b] if isinstance(vals[b], float) else vals[b][None]
        out = jnp.where(s3 == keys[b][None], v, out)
    return out.reshape(rows, tn)


def _peer_kernel(n_slabs, h2t_ref, x1_ref, wq_ref, keys_ref, u_ref, vt_ref, gf_ref, y_ref,
                 s1_ref, s2_ref, cnt_ref, e1_ref, rank2_ref, e2_ref, coef0_ref, coef1_ref, gate0_ref, gate1_ref,
                 acc_ref):
    j = pl.program_id(1)
    ns = n_slabs
    coef_refs = (coef0_ref, coef1_ref)
    gate_refs = (gate0_ref, gate1_ref)
    tn = h2t_ref.shape[1]
    ht = h2t_ref[...]

    @pl.when(j == 0)
    def _():
        qt = jnp.dot(wq_ref[...], ht, preferred_element_type=F32).astype(BF16)
        for p, dst in ((0, s1_ref), (1, s2_ref)):
            for hh in range(PEER_HEADS):
                r0 = (p * PEER_HEADS + hh) * PEER_HALF
                dst[hh] = jnp.dot(keys_ref[p, hh], qt[r0:r0 + PEER_HALF, :], preferred_element_type=F32)

        lw = 128

        def selection_tables(ci, _):
            lanes = pl.ds(pl.multiple_of(ci * lw, lw), lw)
            sub = lax.broadcasted_iota(jnp.int32, (8, lw), 0)
            v1 = [jnp.zeros((8, lw), F32) for _ in range(PEER_TOPK)]
            v2 = [jnp.zeros((8, lw), F32) for _ in range(PEER_TOPK)]
            ranks = [float(b) for b in range(PEER_TOPK)]
            for hh in range(PEER_HEADS):
                top = _top16_rows(s1_ref[hh, :, lanes])
                v1 = [jnp.where(sub == hh, top[a], v1[a]) for a in range(PEER_TOPK)]
                s2 = s2_ref[hh, :, lanes]
                top = _top16_rows(s2)
                v2 = [jnp.where(sub == hh, top[a], v2[a]) for a in range(PEER_TOPK)]
                rank2_ref[hh, :, lanes] = _lookup_desc(top, ranks, float(PEER_TOPK), s2).astype(BF16)
                e2 = jnp.exp(s2.reshape(N_KEYS // 8, 8, lw) - top[0][None])
                e2_ref[hh, :, lanes] = e2.reshape(N_KEYS, lw).astype(BF16)
            cand_rows = [[v1[a] + v2[b] for b in range(PEER_TOPK // (a + 1))] for a in range(PEER_TOPK)]
            cands = [c for row in cand_rows for c in row]
            pad = [jnp.full((8, lw), NEG_BIG, F32)] * (-len(cands) % PEER_TOPK)
            padded = cands + pad
            lists = [_sort_desc(padded[i:i + PEER_TOPK]) for i in range(0, len(padded), PEER_TOPK)]
            while len(lists) > 1:
                lists = [_top_merge(lists[i], lists[i + 1]) for i in range(0, len(lists), 2)]
            tau = lists[0][PEER_TOPK - 1]
            m = cands[0]
            z = jnp.zeros((8, lw), F32)
            for cnd in cands:
                z = z + jnp.where(cnd >= tau, jnp.exp(cnd - m), 0.0)
            shift = v1[0] + jnp.log(z)
            cnt_rows = []
            for row in cand_rows:
                n_sel = jnp.zeros((8, lw), F32)
                for cnd in row:
                    n_sel = n_sel + jnp.where(cnd >= tau, 1.0, 0.0)
                cnt_rows.append(n_sel)
            for hh in range(PEER_HEADS):
                s1 = s1_ref[hh, :, lanes]
                head_row = lambda x: jnp.broadcast_to(x[hh:hh + 1, :], (8, lw))
                cnt = _lookup_desc([head_row(v) for v in v1], [head_row(c) for c in cnt_rows], 0.0, s1)
                cnt_ref[hh, :, lanes] = _twin_bf16_words(cnt)
                e1_ref[hh, :, lanes] = _twin_bf16_words(jnp.exp(s1 - shift[hh:hh + 1, :]))
            return 0

        lax.fori_loop(0, tn // lw, selection_tables, 0)
        acc_ref[...] = jnp.zeros_like(acc_ref)
        coef_refs[1][...] = jnp.zeros_like(coef_refs[1])

    n_i1 = u_ref.shape[0] // N_KEYS
    pack = 16

    def gate_group(dst_ref, slab, ii):
        i1 = jnp.minimum(slab * n_i1 + ii, N_KEYS - 1)
        as_tile = lambda row: pltpu.bitcast(jnp.broadcast_to(row, (pack // 2, tn)), BF16)
        cnt_b = [as_tile(cnt_ref[hh, pl.ds(i1, 1), :]) for hh in range(PEER_HEADS)]
        e1_b = [as_tile(e1_ref[hh, pl.ds(i1, 1), :]) for hh in range(PEER_HEADS)]
        for c in range(N_KEYS // pack):
            rows = slice(c * pack, (c + 1) * pack)
            g = jnp.zeros((pack, tn), BF16)
            for hh in range(PEER_HEADS):
                w = e2_ref[hh, rows, :] * e1_b[hh]
                g = g + jnp.where(rank2_ref[hh, rows, :] < cnt_b[hh], w, jnp.zeros_like(w))
            r0 = ii * N_KEYS + c * pack
            dst_ref[r0:r0 + pack, :] = g

    @pl.when(j == 0)
    def _():
        for ii in range(n_i1):
            gate_group(gate_refs[0], 0, ii)

    def add_previous_slab(prev_ref):
        acc_ref[...] += jnp.dot(vt_ref[...], prev_ref[...], preferred_element_type=F32)

    def build_slab(parity):
        cur_ref, prev_ref = coef_refs[parity], coef_refs[1 - parity]
        gate_ref, next_gate_ref = gate_refs[parity], gate_refs[1 - parity]
        n_split = 2 if tn % 512 == 0 else 1
        wn = tn // n_split
        m_split = 2
        em = u_ref.shape[0] // m_split
        dm = D_MODEL // m_split
        gm = em // N_KEYS
        assert n_i1 == m_split * gm

        dots = {}

        def expert_dot_quadrant(mi, ni):
            dots[mi, ni] = jnp.dot(u_ref[mi * em:(mi + 1) * em, :], h2t_ref[:, ni * wn:(ni + 1) * wn],
                                   preferred_element_type=F32)

        def previous_slab_quadrant(mi, ni):
            blk = (slice(mi * dm, (mi + 1) * dm), slice(ni * wn, (ni + 1) * wn))
            acc_ref[blk] += jnp.dot(vt_ref[mi * dm:(mi + 1) * dm, :], prev_ref[:, ni * wn:(ni + 1) * wn],
                                    preferred_element_type=F32)

        def coef_group(ii):
            mi, local = divmod(ii, gm)
            for c in range(N_KEYS // pack):
                for ni in range(n_split):
                    lr = local * N_KEYS + c * pack
                    act = _gelu_tanh(dots[mi, ni][lr:lr + pack, :].astype(BF16))
                    r0 = ii * N_KEYS + c * pack
                    cols = slice(ni * wn, (ni + 1) * wn)
                    cur_ref[r0:r0 + pack, cols] = gate_ref[r0:r0 + pack, cols] * act

        quads = [(mi, ni) for mi in range(m_split) for ni in range(n_split)]
        matmuls = ([functools.partial(expert_dot_quadrant, *q) for q in quads]
                   + [functools.partial(previous_slab_quadrant, *q) for q in quads])
        tasks = [[] for _ in matmuls]
        first_coef_slot = n_split
        for ii in range(n_i1):
            slot = min(first_coef_slot + ii // 2, len(tasks) - 1)
            slot = max(slot, (ii // gm + 1) * n_split)
            tasks[min(slot, len(tasks) - 1)].append(functools.partial(coef_group, ii))
        for ii in range(n_i1):
            tasks[ii * len(tasks) // n_i1].append(functools.partial(gate_group, next_gate_ref, j + 1, ii))
        for mm, work in zip(matmuls, tasks):
            mm()
            for w in work:
                w()

    for parity in range(2):
        @pl.when(jnp.logical_and(j < ns, j % 2 == parity))
        def _():
            build_slab(parity)

    @pl.when(j == ns)
    def _():
        add_previous_slab(coef_refs[(n_slabs - 1) % 2])
        x2 = x1_ref[...] + acc_ref[...].T
        y = x2 * lax.rsqrt(jnp.mean(x2 * x2, axis=-1, keepdims=True) + EPS) * gf_ref[...]
        y_ref[...] = y


def _peer(h2t, x1, wq_t, keys_bf, u_bf, vt_bf, norm_f_w):
    n = x1.shape[0]
    tn = _pick_tile(n, (512, 256, 128))
    ns = N_EXPERTS // PEER_SLAB
    table = (PEER_HEADS, N_KEYS, tn)
    return pl.pallas_call(
        functools.partial(_peer_kernel, ns),
        grid=(n // tn, ns + 1),
        in_specs=[pl.BlockSpec((D_MODEL, tn), lambda i, j: (0, i)),
                  pl.BlockSpec((tn, D_MODEL), lambda i, j: (i, 0)),
                  pl.BlockSpec((2 * PEER_HEADS * PEER_HALF, D_MODEL), lambda i, j: (0, 0)),
                  pl.BlockSpec((2, PEER_HEADS, N_KEYS, PEER_HALF), lambda i, j: (0, 0, 0, 0)),
                  pl.BlockSpec((PEER_SLAB, D_MODEL), lambda i, j: (jnp.minimum(j, ns - 1), 0)),
                  pl.BlockSpec((D_MODEL, PEER_SLAB), lambda i, j: (0, jnp.maximum(j - 1, 0))),
                  pl.BlockSpec((1, D_MODEL), lambda i, j: (0, 0))],
        out_specs=pl.BlockSpec((tn, D_MODEL), lambda i, j: (i, 0)),
        out_shape=jax.ShapeDtypeStruct((n, D_MODEL), F32),
        scratch_shapes=[pltpu.VMEM(table, F32),
                        pltpu.VMEM(table, F32),
                        pltpu.VMEM(table, jnp.uint32),
                        pltpu.VMEM(table, jnp.uint32),
                        pltpu.VMEM(table, BF16),
                        pltpu.VMEM(table, BF16),
                        pltpu.VMEM((PEER_SLAB, tn), BF16),
                        pltpu.VMEM((PEER_SLAB, tn), BF16),
                        pltpu.VMEM((PEER_SLAB, tn), BF16),
                        pltpu.VMEM((PEER_SLAB, tn), BF16),
                        pltpu.VMEM((D_MODEL, tn), F32)],
        compiler_params=_params("parallel", "arbitrary"),
        name="peer",
    )(h2t, x1, wq_t, keys_bf, u_bf, vt_bf, norm_f_w.reshape(1, D_MODEL))


def _layer(x, pos, past_k, past_v, s0, wts):
    batch, seq, _ = x.shape
    n = batch * seq
    x2d = x.reshape(n, D_MODEL)
    cos_tab, sin_tab = _rotary_tables(pos)
    tm = _pick_tile(n, (512, 256, 128, 64))
    if seq < tm:
        cos_tab = jnp.tile(cos_tab, (tm // seq, 1))
        sin_tab = jnp.tile(sin_tab, (tm // seq, 1))
    rq, rk, rv, gate, sq, sk, sv, skb, svb, gates = _in_proj(x2d, wts["norm1_w"], wts["w_in"], cos_tab, sin_tab)
    r, s_fin = _retention(rq, rk, rv, gate, s0, wts["ret_gn_w"], batch, seq)
    if past_k is None:
        sbo = _sb_prompt(sq, skb, svb, batch, seq)
    else:
        sbo = _sb_sample(sq, skb, svb, past_k, past_v, batch, seq, past_k.shape[0] // batch)
    x1, h2t = _merge(x2d, r, sbo, gates, wts["w_ret_o"], wts["w_sb_o"], wts["w_out"], wts["norm2_w"])
    y = _peer(h2t, x1, wts["w_pq_t"], wts["pk_keys"], wts["peer_u"], wts["peer_v_t"], wts["norm_f_w"])
    return (y.reshape(batch, seq, D_MODEL), sk.reshape(batch, seq, SB_HEADS, SB_D),
            sv.reshape(batch, seq, SB_HEADS, SB_D), s_fin)


def kernel(x_prompt, x_sample, cache_sb_k, cache_sb_v, state_ret, norm1_w, w_in, ret_gn_w, w_ret_o, w_sb_o,
           w_out, norm2_w, w_pq, pk_keys, peer_u, peer_v, norm_f_w):
    assert norm1_w.shape[0] == 1, "single-layer model"
    wq_t = w_pq[0].reshape(D_MODEL, PEER_HEADS, 2, PEER_HALF).transpose(2, 1, 3, 0)
    wq_t = wq_t.reshape(2 * PEER_HEADS * PEER_HALF, D_MODEL).astype(BF16)
    wts = dict(
        norm1_w=norm1_w[0], w_in=w_in[0].astype(BF16), ret_gn_w=ret_gn_w[0],
        w_ret_o=w_ret_o[0].astype(BF16), w_sb_o=w_sb_o[0].astype(BF16), w_out=w_out[0].astype(BF16),
        norm2_w=norm2_w[0], w_pq_t=wq_t, pk_keys=pk_keys[0].astype(BF16),
        peer_u=peer_u[0].astype(BF16), peer_v_t=peer_v[0].T.astype(BF16), norm_f_w=norm_f_w)
    bp, tp, _ = x_prompt.shape
    bs, ts, _ = x_sample.shape
    past = cache_sb_k.shape[2]
    yp, kp, vp, sp = _layer(x_prompt, jnp.arange(tp, dtype=jnp.int32), None, None,
                            jnp.zeros((bp, RET_HEADS, RET_DK, RET_DV), F32), wts)
    ys, ks, vs, ss = _layer(x_sample, past + jnp.arange(ts, dtype=jnp.int32),
                            cache_sb_k[0].reshape(bs * past, SB_W), cache_sb_v[0].reshape(bs * past, SB_W),
                            state_ret[0], wts)
    return (yp, ys, kp[None], vp[None], sp[None], ks[None], vs[None], ss[None])
```
